```python
import math
import jax, jax.numpy as jnp
from jax import lax
import numpy as np

D_MODEL = 1024
BATCH = 8
SEQ = 4096
DEPTH = 4

N_MEM = 256
N_Q_HEADS = 8
N_KV_HEADS = 2
HEAD_DIM = 64
Q_GROUP = N_Q_HEADS // N_KV_HEADS
WINDOW = 128
BLOCK = 128
ROT_DIM = HEAD_DIM // 4
ROPE_THETA = 500000.0
SSM_WIDTH = D_MODEL // 2
SSM_GROUP = 16
SSM_GROUPS = SSM_WIDTH // SSM_GROUP
SSM_STATE = 64
POOL_WINDOWS = (2, 4, 8, 16)
POOL_WIDTH = D_MODEL // 2
POOL_GROUP = POOL_WIDTH // len(POOL_WINDOWS)
X_HEADS = 4
X_HEAD_DIM = D_MODEL // X_HEADS
D_FF = 2816
N_EXPERTS = 8
TOP_K = 2
D_FF_EXPERT = 3584
N_BRANCHES = 3
Q_WIDTH = N_Q_HEADS * HEAD_DIM
KV_WIDTH = N_KV_HEADS * HEAD_DIM
ATTN_WIDTH = Q_WIDTH
SPLITS = (Q_WIDTH, Q_WIDTH + KV_WIDTH, Q_WIDTH + 2 * KV_WIDTH, Q_WIDTH + 2 * KV_WIDTH + SSM_WIDTH, Q_WIDTH + 2 * KV_WIDTH + SSM_WIDTH + POOL_WIDTH)
IN_WIDTH = SPLITS[-1] + N_BRANCHES * D_MODEL
DN_ALPHA = (2.0 * DEPTH) ** 0.25
DN_BETA = (8.0 * DEPTH) ** -0.25
N_DENSE = (DEPTH + 1) // 2
N_MOE = DEPTH // 2
LN_EPS = 1e-5
NEG_INF = -1e30

kernel_name = 'hybrid_gated_swa_s5_pool_moe_block'


def layer_norm(x, g, b):
    xf = x.astype(jnp.float32)
    mu = jnp.mean(xf, axis=-1, keepdims=True)
    var = jnp.mean(jnp.square(xf - mu), axis=-1, keepdims=True)
    y = (xf - mu) * lax.rsqrt(var + LN_EPS)
    return (y * g.astype(jnp.float32) + b.astype(jnp.float32)).astype(x.dtype)


def partial_rotary(t, pos):
    half = ROT_DIM // 2
    inv_freq = jnp.power(jnp.float32(ROPE_THETA), -jnp.arange(half, dtype=jnp.float32) / half)
    ang = pos.astype(jnp.float32)[:, None] * inv_freq[None, :]
    cos = jnp.cos(ang)[None, :, None, :]
    sin = jnp.sin(ang)[None, :, None, :]
    tr = t[..., :ROT_DIM].astype(jnp.float32)
    t1, t2 = tr[..., :half], tr[..., half:]
    rot = jnp.concatenate([t1 * cos - t2 * sin, t2 * cos + t1 * sin], axis=-1)
    return jnp.concatenate([rot.astype(t.dtype), t[..., ROT_DIM:]], axis=-1)


def sliding_window_attention(q, k, v, sinks):
    bsz, l = q.shape[0], q.shape[1]
    nb = l // BLOCK
    qb = q.reshape(bsz, nb, BLOCK, N_KV_HEADS, Q_GROUP, HEAD_DIM)
    kb = k.reshape(bsz, nb, BLOCK, N_KV_HEADS, HEAD_DIM)
    vb = v.reshape(bsz, nb, BLOCK, N_KV_HEADS, HEAD_DIM)

    def with_prev_block(t):
        prev = jnp.pad(t, ((0, 0), (1, 0), (0, 0), (0, 0), (0, 0)))[:, :nb]
        return jnp.concatenate([prev, t], axis=2)

    kw, vw = with_prev_block(kb), with_prev_block(vb)
    s = jnp.einsum('bnqkgd,bnskd->bnkgqs', qb, kw).astype(jnp.float32) * (HEAD_DIM ** -0.5)
    qi = jnp.arange(BLOCK)[:, None]
    si = jnp.arange(2 * BLOCK)[None, :]
    rel = qi + BLOCK - si
    band = (rel >= 0) & (rel < WINDOW)
    has_prev = (jnp.arange(nb)[:, None] > 0) | (si >= BLOCK)
    mask = band[None, :, :] & has_prev[:, None, :]
    s = jnp.where(mask[None, :, None, None], s, NEG_INF)
    sink = sinks.astype(jnp.float32).reshape(N_KV_HEADS, Q_GROUP)[None, None, :, :, None, None]
    m = jnp.maximum(jnp.max(s, axis=-1, keepdims=True), sink)
    p = jnp.exp(s - m)
    p = p / (jnp.sum(p, axis=-1, keepdims=True) + jnp.exp(sink - m))
    o = jnp.einsum('bnkgqs,bnskd->bnqkgd', p.astype(v.dtype), vw)
    return o.reshape(bsz, l, N_Q_HEADS * HEAD_DIM)


def _complex_linear_combine(e1, e2):
    a1r, a1i, b1r, b1i = e1
    a2r, a2i, b2r, b2i = e2
    return (a2r * a1r - a2i * a1i, a2r * a1i + a2i * a1r,
            a2r * b1r - a2i * b1i + b2r, a2r * b1i + a2i * b1r + b2i)


def s5_ssm(u, a_re, a_im, log_dt, b_re, b_im, c_re, c_im, d_skip):
    bsz, l = u.shape[0], u.shape[1]
    f32 = jnp.float32
    uf = u.astype(f32).reshape(bsz, l, SSM_GROUPS, SSM_GROUP)
    ar, ai = a_re.astype(f32), a_im.astype(f32)
    dt = jnp.exp(log_dt.astype(f32))[:, None]
    decay = jnp.exp(dt * ar)
    abar_re, abar_im = decay * jnp.cos(dt * ai), decay * jnp.sin(dt * ai)
    inv_abs2 = 1.0 / (ar * ar + ai * ai)
    num_re, num_im = abar_re - 1.0, abar_im
    f_re = (num_re * ar + num_im * ai) * inv_abs2
    f_im = (num_im * ar - num_re * ai) * inv_abs2
    br, bi = b_re.astype(f32), b_im.astype(f32)
    bbar_re = f_re[..., None] * br - f_im[..., None] * bi
    bbar_im = f_re[..., None] * bi + f_im[..., None] * br
    bu_re = jnp.einsum('blgp,gnp->blgn', uf, bbar_re)
    bu_im = jnp.einsum('blgp,gnp->blgn', uf, bbar_im)
    a_seq_re = jnp.broadcast_to(abar_re[None, None], (1, l, SSM_GROUPS, SSM_STATE))
    a_seq_im = jnp.broadcast_to(abar_im[None, None], (1, l, SSM_GROUPS, SSM_STATE))
    _, _, s_re, s_im = lax.associative_scan(_complex_linear_combine, (a_seq_re, a_seq_im, bu_re, bu_im), axis=1)
    y = jnp.einsum('blgn,gpn->blgp', s_re, c_re.astype(f32)) - jnp.einsum('blgn,gpn->blgp', s_im, c_im.astype(f32))
    y = y + d_skip.astype(f32).reshape(SSM_GROUPS, SSM_GROUP) * uf
    return y.reshape(bsz, l, SSM_WIDTH)


def glu_after_gelu(y, w_glu, dtype):
    h = jax.nn.gelu(y).astype(dtype)
    hv, hg = jnp.split(h @ w_glu, 2, axis=-1)
    return hv * jax.nn.sigmoid(hg)


def multiscale_pool(u, pool_w, pool_scale):
    bsz, l = u.shape[0], u.shape[1]
    uf = u.astype(jnp.float32).reshape(bsz, l, len(POOL_WINDOWS), POOL_GROUP)
    cs = jnp.cumsum(uf, axis=1)
    pos1 = jnp.arange(1, l + 1, dtype=jnp.int32)
    outs = []
    for gi, w in enumerate(POOL_WINDOWS):
        csg = cs[:, :, gi]
        prev = jnp.pad(csg[:, :l - w], ((0, 0), (w, 0), (0, 0)))
        count = jnp.minimum(pos1, w).astype(jnp.float32)[None, :, None]
        outs.append((csg - prev) / count - uf[:, :, gi])
    pooled = jnp.stack(outs, axis=2)
    mixed = jnp.einsum('blgc,gcd->blgd', pooled.astype(u.dtype), pool_w)
    return mixed.reshape(bsz, l, POOL_WIDTH) * pool_scale


def hybrid_mixer(x, pos, w_in, b_gate, sinks, a_re, a_im, log_dt, b_re, b_im, c_re, c_im, d_skip, w_glu,
                 pool_w, pool_scale, w_br_attn, w_br_ssm, w_br_pool, w_o):
    bsz, l = x.shape[0], x.shape[1]
    proj = x @ w_in
    q, k, v, u_ssm, u_pool, gate_logits = jnp.split(proj, list(SPLITS), axis=-1)
    q = partial_rotary(q.reshape(bsz, l, N_Q_HEADS, HEAD_DIM), pos)
    k = partial_rotary(k.reshape(bsz, l, N_KV_HEADS, HEAD_DIM), pos)
    v = v.reshape(bsz, l, N_KV_HEADS, HEAD_DIM)
    attn_out = sliding_window_attention(q, k, v, sinks) @ w_br_attn
    ssm_out = glu_after_gelu(s5_ssm(u_ssm, a_re, a_im, log_dt, b_re, b_im, c_re, c_im, d_skip), w_glu, x.dtype) @ w_br_ssm
    pool_out = multiscale_pool(u_pool, pool_w, pool_scale) @ w_br_pool
    gates = jax.nn.sigmoid(gate_logits + b_gate).reshape(bsz, l, N_BRANCHES, D_MODEL)
    merged = gates[:, :, 0] * attn_out + gates[:, :, 1] * ssm_out + gates[:, :, 2] * pool_out
    return merged @ w_o


def memory_cross_attention(x, mem, wq, wkv, wo):
    bsz, l = x.shape[0], x.shape[1]
    m = mem.shape[1]
    q = (x @ wq).reshape(bsz, l, X_HEADS, X_HEAD_DIM)
    kk, vv = jnp.split(mem @ wkv, 2, axis=-1)
    kk = kk.reshape(bsz, m, X_HEADS, X_HEAD_DIM)
    vv = vv.reshape(bsz, m, X_HEADS, X_HEAD_DIM)
    s = jnp.einsum('blhd,bmhd->bhlm', q, kk).astype(jnp.float32) * (X_HEAD_DIM ** -0.5)
    p = jax.nn.softmax(s, axis=-1).astype(x.dtype)
    o = jnp.einsum('bhlm,bmhd->blhd', p, vv).reshape(bsz, l, D_MODEL)
    return o @ wo


def swiglu(x, w_gu, w_down):
    gt, up = jnp.split(x @ w_gu, 2, axis=-1)
    return (jax.nn.silu(gt) * up) @ w_down


def moe_swiglu(x, w_router, b_router, w_gu, w_down):
    logits = (x @ w_router).astype(jnp.float32) + b_router.astype(jnp.float32)
    top_vals, top_idx = lax.top_k(logits, TOP_K)
    top_w = jax.nn.softmax(top_vals, axis=-1)
    combine = jnp.sum(jax.nn.one_hot(top_idx, N_EXPERTS, dtype=jnp.float32) * top_w[..., None], axis=-2)
    out = jnp.zeros_like(x)
    for e in range(N_EXPERTS):
        out = out + combine[..., e:e + 1].astype(x.dtype) * swiglu(x, w_gu[e], w_down[e])
    return out


def setup_inputs(seed: int = 0) -> dict:
    key = jax.random.key(seed)
    ks = jax.random.split(key, 40)

    def nrm(i, shape, scale):
        return jax.random.normal(ks[i], shape, jnp.float32) * scale

    L = DEPTH
    G, N, P = SSM_GROUPS, SSM_STATE, SSM_GROUP
    a_im_base = jnp.pi * jnp.arange(N, dtype=jnp.float32)
    return {
        'x': nrm(0, (BATCH, SEQ, D_MODEL), 1.0),
        'mem': nrm(1, (BATCH, N_MEM, D_MODEL), 1.0),
        'w_in': nrm(2, (L, D_MODEL, IN_WIDTH), D_MODEL ** -0.5),
        'b_gate': nrm(3, (L, N_BRANCHES * D_MODEL), 0.02),
        'attn_sinks': nrm(4, (L, N_Q_HEADS), 0.5),
        'ssm_a_re': -0.5 + nrm(5, (L, G, N), 0.02),
        'ssm_a_im': a_im_base + nrm(6, (L, G, N), 0.02),
        'ssm_log_dt': jax.random.uniform(ks[7], (L, G), jnp.float32, math.log(1e-3), math.log(1e-1)),
        'ssm_b_re': nrm(8, (L, G, N, P), (2.0 * P) ** -0.5),
        'ssm_b_im': nrm(9, (L, G, N, P), (2.0 * P) ** -0.5),
        'ssm_c_re': nrm(10, (L, G, P, N), N ** -0.5),
        'ssm_c_im': nrm(11, (L, G, P, N), N ** -0.5),
        'ssm_d': nrm(12, (L, SSM_WIDTH), 0.5),
        'ssm_w_glu': nrm(13, (L, SSM_WIDTH, 2 * SSM_WIDTH), SSM_WIDTH ** -0.5),
        'pool_w': nrm(14, (L, len(POOL_WINDOWS), POOL_GROUP, POOL_GROUP), POOL_GROUP ** -0.5),
        'pool_scale': 1.0 + nrm(15, (L, POOL_WIDTH), 0.02),
        'w_br_attn': nrm(16, (L, ATTN_WIDTH, D_MODEL), ATTN_WIDTH ** -0.5),
        'w_br_ssm': nrm(17, (L, SSM_WIDTH, D_MODEL), SSM_WIDTH ** -0.5),
        'w_br_pool': nrm(18, (L, POOL_WIDTH, D_MODEL), POOL_WIDTH ** -0.5),
        'w_o': nrm(19, (L, D_MODEL, D_MODEL), D_MODEL ** -0.5 * DN_BETA),
        'ln1_g': 1.0 + nrm(20, (L, D_MODEL), 0.02),
        'ln1_b': nrm(21, (L, D_MODEL), 0.02),
        'xa_wq': nrm(22, (L, D_MODEL, D_MODEL), D_MODEL ** -0.5),
        'xa_wkv': nrm(23, (L, D_MODEL, 2 * D_MODEL), D_MODEL ** -0.5),
        'xa_wo': nrm(24, (L, D_MODEL, D_MODEL), D_MODEL ** -0.5 * DN_BETA),
        'ln2_g': 1.0 + nrm(25, (L, D_MODEL), 0.02),
        'ln2_b': nrm(26, (L, D_MODEL), 0.02),
        'ffn_w_gu': nrm(27, (N_DENSE, D_MODEL, 2 * D_FF), D_MODEL ** -0.5),
        'ffn_w_down': nrm(28, (N_DENSE, D_FF, D_MODEL), D_FF ** -0.5 * DN_BETA),
        'moe_w_router': nrm(29, (N_MOE, D_MODEL, N_EXPERTS), D_MODEL ** -0.5),
        'moe_b_router': nrm(30, (N_MOE, N_EXPERTS), 0.01),
        'moe_w_gu': nrm(31, (N_MOE, N_EXPERTS, D_MODEL, 2 * D_FF_EXPERT), D_MODEL ** -0.5),
        'moe_w_down': nrm(32, (N_MOE, N_EXPERTS, D_FF_EXPERT, D_MODEL), D_FF_EXPERT ** -0.5 * DN_BETA),
        'ln3_g': 1.0 + nrm(33, (L, D_MODEL), 0.02),
        'ln3_b': nrm(34, (L, D_MODEL), 0.02),
    }


def reference(x, mem, w_in, b_gate, attn_sinks, ssm_a_re, ssm_a_im, ssm_log_dt, ssm_b_re, ssm_b_im,
              ssm_c_re, ssm_c_im, ssm_d, ssm_w_glu, pool_w, pool_scale, w_br_attn, w_br_ssm, w_br_pool, w_o,
              ln1_g, ln1_b, xa_wq, xa_wkv, xa_wo, ln2_g, ln2_b, ffn_w_gu, ffn_w_down,
              moe_w_router, moe_b_router, moe_w_gu, moe_w_down, ln3_g, ln3_b):
    pos = jnp.arange(x.shape[1], dtype=jnp.int32)
    for i in range(DEPTH):
        h = hybrid_mixer(x, pos, w_in[i], b_gate[i], attn_sinks[i], ssm_a_re[i], ssm_a_im[i], ssm_log_dt[i],
                         ssm_b_re[i], ssm_b_im[i], ssm_c_re[i], ssm_c_im[i], ssm_d[i], ssm_w_glu[i],
                         pool_w[i], pool_scale[i], w_br_attn[i], w_br_ssm[i], w_br_pool[i], w_o[i])
        x = layer_norm(DN_ALPHA * x + h, ln1_g[i], ln1_b[i])
        c = memory_cross_attention(x, mem, xa_wq[i], xa_wkv[i], xa_wo[i])
        x = layer_norm(DN_ALPHA * x + c, ln2_g[i], ln2_b[i])
        j = i // 2
        if i % 2 == 0:
            f = swiglu(x, ffn_w_gu[j], ffn_w_down[j])
        else:
            f = moe_swiglu(x, moe_w_router[j], moe_b_router[j], moe_w_gu[j], moe_w_down[j])
        x = layer_norm(DN_ALPHA * x + f, ln3_g[i], ln3_b[i])
    return x
```

```python
import functools
import math

import jax
import jax.numpy as jnp
from jax import lax
from jax.experimental import pallas as pl
from jax.experimental.pallas import tpu as pltpu

D_MODEL = 1024
DEPTH = 4
N_Q_HEADS = 8
N_KV_HEADS = 2
HEAD_DIM = 64
WINDOW = 128
ROT_DIM = HEAD_DIM // 4
ROPE_THETA = 500000.0
SSM_WIDTH = D_MODEL // 2
SSM_GROUP = 16
SSM_GROUPS = SSM_WIDTH // SSM_GROUP
SSM_STATE = 64
SSM_LANES = SSM_GROUPS * SSM_STATE
POOL_WINDOWS = (2, 4, 8, 16)
POOL_WIDTH = D_MODEL // 2
POOL_GROUP = POOL_WIDTH // len(POOL_WINDOWS)
POOL_HALO = 16
X_HEADS = 4
X_HEAD_DIM = D_MODEL // X_HEADS
D_FF = 2816
N_EXPERTS = 8
D_FF_EXPERT = 3584
Q_WIDTH = N_Q_HEADS * HEAD_DIM
KV_WIDTH = N_KV_HEADS * HEAD_DIM
DN_ALPHA = (2.0 * DEPTH) ** 0.25
LN_EPS = 1e-5
NEG_INF = -1e30

LANE = 128
VMEM_LIMIT = 56 * 1024 * 1024

BF16 = jnp.bfloat16
F32 = jnp.float32


def _dot(a, b):
    return jnp.dot(a, b, preferred_element_type=F32)


def _dot_nt(a, b):
    return lax.dot_general(a, b, (((1,), (1,)), ((), ())), preferred_element_type=F32)


def _layer_norm(y, g, b):
    mu = jnp.mean(y, axis=-1, keepdims=True)
    d = y - mu
    var = jnp.mean(d * d, axis=-1, keepdims=True)
    return d * lax.rsqrt(var + LN_EPS) * g + b


def _const_spec(shape):
    nd = len(shape)
    return pl.BlockSpec(shape, lambda *_: (0,) * nd)


def _params(sem, limit=VMEM_LIMIT):
    return pltpu.CompilerParams(dimension_semantics=sem, vmem_limit_bytes=limit)


def _rotary(t, cos, sin_a, sin_b):
    return t * cos + pltpu.roll(t, LANE - ROT_DIM // 2, 1) * sin_a + pltpu.roll(t, ROT_DIM // 2, 1) * sin_b


def _inproj_kernel(x_ref, wqkv_ref, wssm_ref, wpool_ref, cos_ref, sa_ref, sb_ref, pw_ref, ps_ref,
                   q_ref, k_ref, v_ref, u_ref, pm_ref, hist_ref, *, tt):
    ti = pl.program_id(1)
    xb = x_ref[...].astype(BF16)
    cos, sa, sb = cos_ref[...], sa_ref[...], sb_ref[...]

    qkv = _dot(xb, wqkv_ref[...])
    for c in range(Q_WIDTH // LANE):
        t = qkv[:, c * LANE:(c + 1) * LANE] * (HEAD_DIM ** -0.5)
        q_ref[:, c * LANE:(c + 1) * LANE] = _rotary(t, cos, sa, sb).astype(BF16)
    k_ref[...] = _rotary(qkv[:, Q_WIDTH:Q_WIDTH + KV_WIDTH], cos, sa, sb).astype(BF16)
    v_ref[...] = qkv[:, Q_WIDTH + KV_WIDTH:].astype(BF16)

    u_ref[...] = _dot(xb, wssm_ref[...])

    @pl.when(ti == 0)
    def _():
        hist_ref[0:POOL_HALO, :] = jnp.zeros((POOL_HALO, POOL_WIDTH), F32)

    up = _dot(xb, wpool_ref[...])
    hist_ref[POOL_HALO:POOL_HALO + tt, :] = up
    pos1 = (ti * tt + 1 + lax.broadcasted_iota(jnp.int32, (tt, 1), 0)).astype(F32)
    for gi, w in enumerate(POOL_WINDOWS):
        lanes = pl.ds(gi * POOL_GROUP, POOL_GROUP)
        acc = hist_ref[pl.ds(POOL_HALO, tt), lanes]
        for j in range(1, w):
            acc = acc + hist_ref[pl.ds(POOL_HALO - j, tt), lanes]
        pooled = acc / jnp.minimum(pos1, float(w)) - up[:, gi * POOL_GROUP:(gi + 1) * POOL_GROUP]
        mixed = _dot(pooled.astype(BF16), pw_ref[gi]) * ps_ref[:, gi * POOL_GROUP:(gi + 1) * POOL_GROUP]
        pm_ref[:, gi * POOL_GROUP:(gi + 1) * POOL_GROUP] = mixed.astype(BF16)
    hist_ref[0:POOL_HALO, :] = hist_ref[tt:tt + POOL_HALO, :]


def _inproj(x, wqkv, wssm, wpool, cos, sa, sb, pool_w, pool_scale, *, tt):
    bsz, seq, d = x.shape
    row = lambda w: pl.BlockSpec((None, tt, w), lambda b, t: (b, t, 0))
    tab = pl.BlockSpec((tt, LANE), lambda b, t: (t, 0))
    return pl.pallas_call(
        functools.partial(_inproj_kernel, tt=tt),
        grid=(bsz, seq // tt),
        in_specs=[row(d), _const_spec(wqkv.shape), _const_spec(wssm.shape), _const_spec(wpool.shape),
                  tab, tab, tab, _const_spec(pool_w.shape), _const_spec(pool_scale.shape)],
        out_specs=[row(Q_WIDTH), row(KV_WIDTH), row(KV_WIDTH), row(SSM_WIDTH), row(POOL_WIDTH)],
        out_shape=[jax.ShapeDtypeStruct((bsz, seq, Q_WIDTH), BF16),
                   jax.ShapeDtypeStruct((bsz, seq, KV_WIDTH), BF16),
                   jax.ShapeDtypeStruct((bsz, seq, KV_WIDTH), BF16),
                   jax.ShapeDtypeStruct((bsz, seq, SSM_WIDTH), F32),
                   jax.ShapeDtypeStruct((bsz, seq, POOL_WIDTH), BF16)],
        scratch_shapes=[pltpu.VMEM((tt + POOL_HALO, POOL_WIDTH), F32)],
        compiler_params=_params(("arbitrary", "arbitrary")),
        name="inproj",
    )(x, wqkv, wssm, wpool, cos, sa, sb, pool_w, pool_scale)


def _half_lanes(a, b, lo):
    return jnp.where(lo, a, b)


def _swa_kernel(sink_ref, q_ref, k_ref, kp_ref, v_ref, vp_ref, o_ref, *, nsub):
    ti = pl.program_id(1)
    blk = WINDOW
    lo = lax.broadcasted_iota(jnp.int32, (2 * blk, LANE), 1) < HEAD_DIM
    qi = lax.broadcasted_iota(jnp.int32, (2 * blk, 2 * blk), 0) % blk
    si = lax.broadcasted_iota(jnp.int32, (2 * blk, 2 * blk), 1)
    rel = qi + blk - si
    band = (rel >= 0) & (rel < WINDOW)
    top = lax.broadcasted_iota(jnp.int32, (2 * blk, 1), 0) < blk
    zero = jnp.zeros((2 * blk, LANE), F32)

    for j in range(nsub):
        if j == 0:
            kc = jnp.concatenate([kp_ref[...], k_ref[0:blk, :]], axis=0)
            vc = jnp.concatenate([vp_ref[...], v_ref[0:blk, :]], axis=0)
            mask = band & ((ti > 0) | (si >= blk))
        else:
            kc = k_ref[(j - 1) * blk:(j + 1) * blk, :]
            vc = v_ref[(j - 1) * blk:(j + 1) * blk, :]
            mask = band
        kc = kc.astype(F32)
        vc = vc.astype(F32)
        kr = pltpu.roll(kc, HEAD_DIM, 1)
        vr = pltpu.roll(vc, HEAD_DIM, 1)
        kmat = [[_half_lanes(kc, zero, lo), _half_lanes(zero, kr, lo)],
                [_half_lanes(kr, zero, lo), _half_lanes(zero, kc, lo)]]
        vmat = [[_half_lanes(vc, zero, lo), _half_lanes(zero, vr, lo)],
                [_half_lanes(vr, zero, lo), _half_lanes(zero, vc, lo)]]
        rows = pl.ds(j * blk, blk)
        for kh in range(N_KV_HEADS):
            base = kh * 2 * LANE
            qq = jnp.concatenate([q_ref[rows, base:base + LANE], q_ref[rows, base + LANE:base + 2 * LANE]], axis=0)
            probs = []
            for par in range(2):
                s = _dot_nt(qq, kmat[kh][par].astype(BF16))
                s = jnp.where(mask, s, NEG_INF)
                h_top = kh * 4 + par
                sink = jnp.where(top, sink_ref[h_top], sink_ref[h_top + 2])
                m = jnp.maximum(jnp.max(s, axis=-1, keepdims=True), sink)
                p = jnp.exp(s - m)
                p = p / (jnp.sum(p, axis=-1, keepdims=True) + jnp.exp(sink - m))
                probs.append(p.astype(BF16))
            pcat = jnp.concatenate(probs, axis=1)
            vcat = jnp.concatenate([vmat[kh][0], vmat[kh][1]], axis=0).astype(BF16)
            o = _dot(pcat, vcat)
            o_ref[rows, base:base + LANE] = o[0:blk].astype(BF16)
            o_ref[rows, base + LANE:base + 2 * LANE] = o[blk:2 * blk].astype(BF16)


def _swa(q, k, v, sinks, *, tq):
    bsz, seq, _ = q.shape
    nsub = tq // WINDOW
    cur = lambda w: pl.BlockSpec((None, tq, w), lambda b, t: (b, t, 0))
    prev = pl.BlockSpec((None, WINDOW, KV_WIDTH), lambda b, t: (b, jnp.maximum(t * nsub - 1, 0), 0))
    return pl.pallas_call(
        functools.partial(_swa_kernel, nsub=nsub),
        grid=(bsz, seq // tq),
        in_specs=[pl.BlockSpec(memory_space=pltpu.SMEM), cur(Q_WIDTH), cur(KV_WIDTH), prev, cur(KV_WIDTH), prev],
        out_specs=cur(Q_WIDTH),
        out_shape=jax.ShapeDtypeStruct((bsz, seq, Q_WIDTH), BF16),
        compiler_params=_params(("parallel", "parallel")),
        name="swa",
    )(sinks, q, k, k, v, v)


SSM_CHUNKS = 4
SSM_CHUNK_LANES = SSM_LANES // SSM_CHUNKS
SSM_CHUNK_IN = SSM_WIDTH // SSM_CHUNKS
SCAN_LANES = 512


def _ssm_kernel(u_ref, bw_ref, cw_ref, are_ref, aim_ref, d_ref, wglu_ref, o_ref, xs_ref, st_ref, h_ref,
                *, tc, bsz):
    @pl.when(pl.program_id(0) == 0)
    def _():
        h_ref[...] = jnp.zeros(h_ref.shape, F32)

    for b in range(bsz):
        for c in range(SSM_CHUNKS):
            xs_ref[c, pl.ds(b, tc, stride=bsz), :] = u_ref[b, :, c * SSM_CHUNK_IN:(c + 1) * SSM_CHUNK_IN]

    for part in range(2):
        for c in range(SSM_CHUNKS):
            col = part * SSM_LANES + c * SSM_CHUNK_LANES
            st_ref[:, col:col + SSM_CHUNK_LANES] = _dot(xs_ref[c].astype(BF16), bw_ref[part * SSM_CHUNKS + c])

    for jb in range(SSM_LANES // SCAN_LANES):
        re = pl.ds(jb * SCAN_LANES, SCAN_LANES)
        im = pl.ds(SSM_LANES + jb * SCAN_LANES, SCAN_LANES)
        ar = are_ref[:, re]
        ai = aim_ref[:, re]

        def step(t, carry):
            hr, hi = carry
            r = pl.ds(pl.multiple_of(t * bsz, bsz), bsz)
            nr = ar * hr - ai * hi + st_ref[r, re]
            ni = ar * hi + ai * hr + st_ref[r, im]
            st_ref[r, re] = nr
            st_ref[r, im] = ni
            return nr, ni

        hr, hi = lax.fori_loop(0, tc, step, (h_ref[:, re], h_ref[:, im]), unroll=8)
        h_ref[:, re] = hr
        h_ref[:, im] = hi

    ys = []
    for c in range(SSM_CHUNKS):
        sre = st_ref[:, c * SSM_CHUNK_LANES:(c + 1) * SSM_CHUNK_LANES].astype(BF16)
        sim = st_ref[:, SSM_LANES + c * SSM_CHUNK_LANES:SSM_LANES + (c + 1) * SSM_CHUNK_LANES].astype(BF16)
        cols = slice(c * SSM_CHUNK_IN, (c + 1) * SSM_CHUNK_IN)
        y = _dot(sre, cw_ref[c]) + _dot(sim, cw_ref[SSM_CHUNKS + c])
        ys.append(y + d_ref[:, cols] * xs_ref[c])

    hh = jax.nn.gelu(jnp.concatenate(ys, axis=1)).astype(BF16)
    z = _dot(hh, wglu_ref[...])
    res = z[:, :SSM_WIDTH] * jax.nn.sigmoid(z[:, SSM_WIDTH:])
    for c in range(SSM_CHUNKS):
        xs_ref[c] = res[:, c * SSM_CHUNK_IN:(c + 1) * SSM_CHUNK_IN]
    for b in range(bsz):
        for c in range(SSM_CHUNKS):
            o_ref[b, :, c * SSM_CHUNK_IN:(c + 1) * SSM_CHUNK_IN] = (
                xs_ref[c, pl.ds(b, tc, stride=bsz), :].astype(BF16))


def _ssm(u, bw, cw, are, aim, d_skip, wglu, *, tc):
    bsz, seq, _ = u.shape
    blk = pl.BlockSpec((bsz, tc, SSM_WIDTH), lambda t: (0, t, 0))
    return pl.pallas_call(
        functools.partial(_ssm_kernel, tc=tc, bsz=bsz),
        grid=(seq // tc,),
        in_specs=[blk, _const_spec(bw.shape), _const_spec(cw.shape), _const_spec(are.shape),
                  _const_spec(aim.shape), _const_spec(d_skip.shape), _const_spec(wglu.shape)],
        out_specs=blk,
        out_shape=jax.ShapeDtypeStruct((bsz, seq, SSM_WIDTH), BF16),
        scratch_shapes=[pltpu.VMEM((SSM_CHUNKS, tc * bsz, SSM_CHUNK_IN), F32),
                        pltpu.VMEM((tc * bsz, 2 * SSM_LANES), F32),
                        pltpu.VMEM((bsz, 2 * SSM_LANES), F32)],
        compiler_params=_params(("arbitrary",)),
        name="ssm",
    )(u, bw, cw, are, aim, d_skip, wglu)


def _ssm_weights(a_re, a_im, log_dt, b_re, b_im, c_re, c_im, bsz):
    dt = jnp.exp(log_dt)[:, None]
    decay = jnp.exp(dt * a_re)
    abar_re, abar_im = decay * jnp.cos(dt * a_im), decay * jnp.sin(dt * a_im)
    inv_abs2 = 1.0 / (a_re * a_re + a_im * a_im)
    num_re, num_im = abar_re - 1.0, abar_im
    f_re = (num_re * a_re + num_im * a_im) * inv_abs2
    f_im = (num_im * a_re - num_re * a_im) * inv_abs2
    bbar_re = f_re[..., None] * b_re - f_im[..., None] * b_im
    bbar_im = f_re[..., None] * b_im + f_im[..., None] * b_re
    gpc = SSM_GROUPS // SSM_CHUNKS
    eye = jnp.eye(gpc, dtype=F32)

    def in_chunks(bb):
        t = bb.reshape(SSM_CHUNKS, gpc, SSM_STATE, SSM_GROUP)
        return jnp.einsum('cgnp,gh->cgphn', t, eye).reshape(SSM_CHUNKS, gpc * SSM_GROUP, gpc * SSM_STATE)

    def out_chunks(cc):
        t = cc.reshape(SSM_CHUNKS, gpc, SSM_GROUP, SSM_STATE)
        return jnp.einsum('cgpn,gh->cgnhp', t, eye).reshape(SSM_CHUNKS, gpc * SSM_STATE, gpc * SSM_GROUP)

    bw = jnp.concatenate([in_chunks(bbar_re), in_chunks(bbar_im)], axis=0).astype(BF16)
    cw = jnp.concatenate([out_chunks(c_re), out_chunks(-c_im)], axis=0).astype(BF16)
    are = jnp.broadcast_to(abar_re.reshape(1, SSM_LANES), (bsz, SSM_LANES))
    aim = jnp.broadcast_to(abar_im.reshape(1, SSM_LANES), (bsz, SSM_LANES))
    return bw, cw, are, aim


def _merge_kernel(x_ref, a_ref, s_ref, p_ref, wg_ref, bg_ref, wa_ref, ws_ref, wp_ref, wo_ref, g_ref, b_ref, o_ref):
    x = x_ref[...]
    xb = x.astype(BF16)
    merged = None
    for i, (br_ref, w_ref) in enumerate(((a_ref, wa_ref), (s_ref, ws_ref), (p_ref, wp_ref))):
        cols = slice(i * D_MODEL, (i + 1) * D_MODEL)
        gate = jax.nn.sigmoid(_dot(xb, wg_ref[:, cols]) + bg_ref[:, cols])
        term = gate * _dot(br_ref[...], w_ref[...])
        merged = term if merged is None else merged + term
    h = _dot(merged.astype(BF16), wo_ref[...])
    o_ref[...] = _layer_norm(DN_ALPHA * x + h, g_ref[...], b_ref[...])


def _merge(x, a, s, p, wg, bg, wa, ws, wp, wo, g, b, *, tm):
    n, d = x.shape
    row = lambda w: pl.BlockSpec((tm, w), lambda i: (i, 0))
    consts = (wg, bg, wa, ws, wp, wo, g, b)
    return pl.pallas_call(
        _merge_kernel,
        grid=(n // tm,),
        in_specs=[row(d), row(Q_WIDTH), row(SSM_WIDTH), row(POOL_WIDTH)] + [_const_spec(c.shape) for c in consts],
        out_specs=row(d),
        out_shape=jax.ShapeDtypeStruct((n, d), F32),
        compiler_params=_params(("parallel",)),
        name="merge",
    )(x, a, s, p, *consts)


def _matmul_kernel(a_ref, w_ref, o_ref):
    o_ref[...] = _dot(a_ref[...].astype(BF16), w_ref[...]).astype(o_ref.dtype)


def _matmul(a, w, *, tm, out_dtype):
    m, k = a.shape
    n = w.shape[1]
    return pl.pallas_call(
        _matmul_kernel,
        grid=(m // tm,),
        in_specs=[pl.BlockSpec((tm, k), lambda i: (i, 0)), _const_spec(w.shape)],
        out_specs=pl.BlockSpec((tm, n), lambda i: (i, 0)),
        out_shape=jax.ShapeDtypeStruct((m, n), out_dtype),
        compiler_params=_params(("parallel",)),
        name="kvproj",
    )(a, w)


def _xattn_kernel(x_ref, k_ref, v_ref, wq_ref, wo_ref, g_ref, b_ref, o_ref):
    x = x_ref[...]
    q = (_dot(x.astype(BF16), wq_ref[...]) * (X_HEAD_DIM ** -0.5)).astype(BF16)
    outs = []
    for h in range(X_HEADS):
        cols = slice(h * X_HEAD_DIM, (h + 1) * X_HEAD_DIM)
        s = _dot_nt(q[:, cols], k_ref[:, cols])
        m = jnp.max(s, axis=-1, keepdims=True)
        p = jnp.exp(s - m)
        p = p / jnp.sum(p, axis=-1, keepdims=True)
        outs.append(_dot(p.astype(BF16), v_ref[:, cols]).astype(BF16))
    c = _dot(jnp.concatenate(outs, axis=1), wo_ref[...])
    o_ref[...] = _layer_norm(DN_ALPHA * x + c, g_ref[...], b_ref[...])


def _xattn(x, kv, wq, wo, g, b, *, tm):
    bsz, seq, d = x.shape
    nmem = kv.shape[1]
    row = pl.BlockSpec((None, tm, d), lambda bi, t: (bi, t, 0))
    kspec = pl.BlockSpec((None, nmem, d), lambda bi, t: (bi, 0, 0))
    vspec = pl.BlockSpec((None, nmem, d), lambda bi, t: (bi, 0, 1))
    consts = (wq, wo, g, b)
    return pl.pallas_call(
        _xattn_kernel,
        grid=(bsz, seq // tm),
        in_specs=[row, kspec, vspec] + [_const_spec(c.shape) for c in consts],
        out_specs=row,
        out_shape=jax.ShapeDtypeStruct((bsz, seq, d), F32),
        compiler_params=_params(("parallel", "parallel")),
        name="xattn",
    )(x, kv, kv, *consts)


def _ffn_kernel(x_ref, wg_ref, wu_ref, wd_ref, g_ref, b_ref, o_ref, *, nchunk):
    x = x_ref[...]
    xb = x.astype(BF16)
    fc = wg_ref.shape[1] // nchunk
    acc = None
    for c in range(nchunk):
        cols = slice(c * fc, (c + 1) * fc)
        hid = jax.nn.silu(_dot(xb, wg_ref[:, cols])) * _dot(xb, wu_ref[:, cols])
        part = _dot(hid.astype(BF16), wd_ref[cols, :])
        acc = part if acc is None else acc + part
    o_ref[...] = _layer_norm(DN_ALPHA * x + acc, g_ref[...], b_ref[...])


def _ffn(x, wg, wu, wd, g, b, *, tm, nchunk):
    n, d = x.shape
    row = pl.BlockSpec((tm, d), lambda i: (i, 0))
    consts = (wg, wu, wd, g, b)
    return pl.pallas_call(
        functools.partial(_ffn_kernel, nchunk=nchunk),
        grid=(n // tm,),
        in_specs=[row] + [_const_spec(c.shape) for c in consts],
        out_specs=row,
        out_shape=jax.ShapeDtypeStruct((n, d), F32),
        compiler_params=_params(("parallel",)),
        name="ffn",
    )(x, *consts)


def _router_kernel(x_ref, wr_ref, br_ref, o_ref):
    logits = lax.dot_general(wr_ref[...], x_ref[...], (((1,), (1,)), ((), ())),
                             precision=lax.Precision.HIGHEST, preferred_element_type=F32) + br_ref[...]
    eid = lax.broadcasted_iota(jnp.int32, logits.shape, 0)
    v1 = jnp.max(logits, axis=0, keepdims=True)
    e1 = jnp.min(jnp.where(logits == v1, eid, N_EXPERTS), axis=0, keepdims=True)
    rest = jnp.where(eid == e1, -jnp.inf, logits)
    v2 = jnp.max(rest, axis=0, keepdims=True)
    e2 = jnp.min(jnp.where(rest == v2, eid, N_EXPERTS), axis=0, keepdims=True)
    t = jnp.exp(v2 - v1)
    w1 = 1.0 / (1.0 + t)
    w2 = t / (1.0 + t)
    o_ref[...] = jnp.where(eid == e1, w1, 0.0) + jnp.where(eid == e2, w2, 0.0)


def _router(x, wr_t, br, *, tm):
    n, d = x.shape
    return pl.pallas_call(
        _router_kernel,
        grid=(n // tm,),
        in_specs=[pl.BlockSpec((tm, d), lambda i: (i, 0)), _const_spec(wr_t.shape), _const_spec(br.shape)],
        out_specs=pl.BlockSpec((N_EXPERTS, tm), lambda i: (0, i)),
        out_shape=jax.ShapeDtypeStruct((N_EXPERTS, n), F32),
        compiler_params=_params(("parallel",)),
        name="router",
    )(x, wr_t, br)


def _moe_dense_kernel(x_ref, cw_ref, wg_ref, wu_ref, wd_ref, g_ref, b_ref, o_ref, acc_ref):
    e, c = pl.program_id(1), pl.program_id(2)

    @pl.when((e == 0) & (c == 0))
    def _():
        acc_ref[...] = jnp.zeros(acc_ref.shape, F32)

    xb = x_ref[...].astype(BF16)
    hid = jax.nn.silu(_dot(xb, wg_ref[...])) * _dot(xb, wu_ref[...])
    acc_ref[...] += cw_ref[...] * _dot(hid.astype(BF16), wd_ref[...])

    @pl.when((e == pl.num_programs(1) - 1) & (c == pl.num_programs(2) - 1))
    def _():
        o_ref[...] = _layer_norm(DN_ALPHA * x_ref[...] + acc_ref[...], g_ref[...], b_ref[...])


def _moe_dense(x, combine, wg, wu, wd, g, b, *, tm, nchunk):
    n, d = x.shape
    fc = wg.shape[2] // nchunk
    row = pl.BlockSpec((tm, d), lambda i, e, c: (i, 0))
    return pl.pallas_call(
        _moe_dense_kernel,
        grid=(n // tm, N_EXPERTS, nchunk),
        in_specs=[row,
                  pl.BlockSpec((None, tm, 1), lambda i, e, c: (e, i, 0)),
                  pl.BlockSpec((None, d, fc), lambda i, e, c: (e, 0, c)),
                  pl.BlockSpec((None, d, fc), lambda i, e, c: (e, 0, c)),
                  pl.BlockSpec((None, fc, d), lambda i, e, c: (e, c, 0)),
                  _const_spec(g.shape), _const_spec(b.shape)],
        out_specs=row,
        out_shape=jax.ShapeDtypeStruct((n, d), F32),
        scratch_shapes=[pltpu.VMEM((tm, d), F32)],
        compiler_params=_params(("parallel", "arbitrary", "arbitrary")),
        name="moe",
    )(x, combine, wg, wu, wd, g, b)


def _moe(x, w_router, b_router, w_gu, w_down, g, b):
    n = x.shape[0]
    combine = _router(x, w_router.T, b_router.reshape(N_EXPERTS, 1), tm=512)
    wgu = w_gu.astype(BF16)
    return _moe_dense(x, combine.reshape(N_EXPERTS, n, 1), wgu[:, :, :D_FF_EXPERT], wgu[:, :, D_FF_EXPERT:],
                      w_down.astype(BF16), g, b, tm=1024, nchunk=4)


def _rotary_tables(seq):
    half = ROT_DIM // 2
    inv_freq = jnp.power(jnp.float32(ROPE_THETA), -jnp.arange(half, dtype=F32) / half)
    ang = jnp.arange(seq, dtype=jnp.int32).astype(F32)[:, None] * inv_freq[None, :]
    cos, sin = jnp.cos(ang), jnp.sin(ang)
    d = jnp.arange(LANE) % HEAD_DIM
    first, second = d < half, (d >= half) & (d < ROT_DIM)
    idx = d % half
    cos_t = jnp.where((first | second)[None, :], cos[:, idx], 1.0)
    sin_a = jnp.where(first[None, :], -sin[:, idx], 0.0)
    sin_b = jnp.where(second[None, :], sin[:, idx], 0.0)
    return cos_t, sin_a, sin_b


def kernel(x, mem, w_in, b_gate, attn_sinks, ssm_a_re, ssm_a_im, ssm_log_dt, ssm_b_re, ssm_b_im, ssm_c_re, ssm_c_im, ssm_d, ssm_w_glu, pool_w, pool_scale, w_br_attn, w_br_ssm, w_br_pool, w_o, ln1_g, ln1_b, xa_wq, xa_wkv, xa_wo, ln2_g, ln2_b, ffn_w_gu, ffn_w_down, moe_w_router, moe_b_router, moe_w_gu, moe_w_down, ln3_g, ln3_b):
    bsz, seq, d = x.shape
    n = bsz * seq
    nmem = mem.shape[1]
    cos_t, sin_a, sin_b = _rotary_tables(seq)
    row2 = lambda v: v.reshape(1, -1)
    qkv_w = Q_WIDTH + 2 * KV_WIDTH
    mem2 = mem.reshape(bsz * nmem, d)

    for i in range(DEPTH):
        wi = w_in[i].astype(BF16)
        wqkv, wssm = wi[:, :qkv_w], wi[:, qkv_w:qkv_w + SSM_WIDTH]
        wpool = wi[:, qkv_w + SSM_WIDTH:qkv_w + SSM_WIDTH + POOL_WIDTH]
        wgate = wi[:, qkv_w + SSM_WIDTH + POOL_WIDTH:]
        q, k, v, u, pm = _inproj(x, wqkv, wssm, wpool, cos_t, sin_a, sin_b, pool_w[i].astype(BF16),
                                 row2(pool_scale[i]), tt=512)
        attn = _swa(q, k, v, attn_sinks[i], tq=512)
        bw, cw, are, aim = _ssm_weights(ssm_a_re[i], ssm_a_im[i], ssm_log_dt[i], ssm_b_re[i], ssm_b_im[i],
                                        ssm_c_re[i], ssm_c_im[i], bsz)
        hs = _ssm(u, bw, cw, are, aim, row2(ssm_d[i]), ssm_w_glu[i].astype(BF16), tc=128)
        x2 = _merge(x.reshape(n, d), attn.reshape(n, -1), hs.reshape(n, -1), pm.reshape(n, -1),
                    wgate, row2(b_gate[i]), w_br_attn[i].astype(BF16), w_br_ssm[i].astype(BF16),
                    w_br_pool[i].astype(BF16), w_o[i].astype(BF16), row2(ln1_g[i]), row2(ln1_b[i]), tm=512)

        kv = _matmul(mem2, xa_wkv[i].astype(BF16), tm=512, out_dtype=BF16).reshape(bsz, nmem, 2 * d)
        x3 = _xattn(x2.reshape(bsz, seq, d), kv, xa_wq[i].astype(BF16), xa_wo[i].astype(BF16),
                    row2(ln2_g[i]), row2(ln2_b[i]), tm=512).reshape(n, d)

        j = i // 2
        if i % 2 == 0:
            wgu = ffn_w_gu[j].astype(BF16)
            x4 = _ffn(x3, wgu[:, :D_FF], wgu[:, D_FF:], ffn_w_down[j].astype(BF16),
                      row2(ln3_g[i]), row2(ln3_b[i]), tm=512, nchunk=2)
        else:
            x4 = _moe(x3, moe_w_router[j], moe_b_router[j], moe_w_gu[j], moe_w_down[j],
                      row2(ln3_g[i]), row2(ln3_b[i]))
        x = x4.reshape(bsz, seq, d)
    return x
```

```python
import functools
import math

import jax
import jax.numpy as jnp
from jax import lax
from jax.experimental import pallas as pl
from jax.experimental.pallas import tpu as pltpu

D_MODEL = 1024
DEPTH = 4
N_Q_HEADS = 8
N_KV_HEADS = 2
HEAD_DIM = 64
WINDOW = 128
ROT_DIM = HEAD_DIM // 4
ROPE_THETA = 500000.0
SSM_WIDTH = D_MODEL // 2
SSM_GROUP = 16
SSM_GROUPS = SSM_WIDTH // SSM_GROUP
SSM_STATE = 64
SSM_LANES = SSM_GROUPS * SSM_STATE
POOL_WINDOWS = (2, 4, 8, 16)
POOL_WIDTH = D_MODEL // 2
POOL_GROUP = POOL_WIDTH // len(POOL_WINDOWS)
POOL_HALO = 16
X_HEADS = 4
X_HEAD_DIM = D_MODEL // X_HEADS
D_FF = 2816
N_EXPERTS = 8
D_FF_EXPERT = 3584
Q_WIDTH = N_Q_HEADS * HEAD_DIM
KV_WIDTH = N_KV_HEADS * HEAD_DIM
DN_ALPHA = (2.0 * DEPTH) ** 0.25
LN_EPS = 1e-5
NEG_INF = -1e30

LANE = 128
VMEM_LIMIT = 56 * 1024 * 1024

BF16 = jnp.bfloat16
F32 = jnp.float32


def _dot(a, b):
    return jnp.dot(a, b, preferred_element_type=F32)


def _dot_nt(a, b):
    return lax.dot_general(a, b, (((1,), (1,)), ((), ())), preferred_element_type=F32)


def _layer_norm(y, g, b):
    mu = jnp.mean(y, axis=-1, keepdims=True)
    d = y - mu
    var = jnp.mean(d * d, axis=-1, keepdims=True)
    return d * lax.rsqrt(var + LN_EPS) * g + b


def _const_spec(shape):
    nd = len(shape)
    return pl.BlockSpec(shape, lambda *_: (0,) * nd)


def _params(sem, limit=VMEM_LIMIT):
    return pltpu.CompilerParams(dimension_semantics=sem, vmem_limit_bytes=limit)


def _rotary(t, cos, sin_a, sin_b):
    return t * cos + pltpu.roll(t, LANE - ROT_DIM // 2, 1) * sin_a + pltpu.roll(t, ROT_DIM // 2, 1) * sin_b


def _inproj_kernel(x_ref, wqkv_ref, wssm_ref, wpool_ref, cos_ref, sa_ref, sb_ref, pw_ref, ps_ref,
                   q_ref, k_ref, v_ref, u_ref, pm_ref, hist_ref, *, tt):
    ti = pl.program_id(1)
    xb = x_ref[...].astype(BF16)
    cos, sa, sb = cos_ref[...], sa_ref[...], sb_ref[...]

    qkv = _dot(xb, wqkv_ref[...])
    for c in range(Q_WIDTH // LANE):
        t = qkv[:, c * LANE:(c + 1) * LANE] * (HEAD_DIM ** -0.5)
        q_ref[:, c * LANE:(c + 1) * LANE] = _rotary(t, cos, sa, sb).astype(BF16)
    k_ref[...] = _rotary(qkv[:, Q_WIDTH:Q_WIDTH + KV_WIDTH], cos, sa, sb).astype(BF16)
    v_ref[...] = qkv[:, Q_WIDTH + KV_WIDTH:].astype(BF16)

    u_ref[...] = _dot(xb, wssm_ref[...])

    @pl.when(ti == 0)
    def _():
        hist_ref[0:POOL_HALO, :] = jnp.zeros((POOL_HALO, POOL_WIDTH), F32)

    up = _dot(xb, wpool_ref[...])
    hist_ref[POOL_HALO:POOL_HALO + tt, :] = up
    pos1 = (ti * tt + 1 + lax.broadcasted_iota(jnp.int32, (tt, 1), 0)).astype(F32)
    for gi, w in enumerate(POOL_WINDOWS):
        lanes = pl.ds(gi * POOL_GROUP, POOL_GROUP)
        acc = hist_ref[pl.ds(POOL_HALO, tt), lanes]
        for j in range(1, w):
            acc = acc + hist_ref[pl.ds(POOL_HALO - j, tt), lanes]
        pooled = acc / jnp.minimum(pos1, float(w)) - up[:, gi * POOL_GROUP:(gi + 1) * POOL_GROUP]
        mixed = _dot(pooled.astype(BF16), pw_ref[gi]) * ps_ref[:, gi * POOL_GROUP:(gi + 1) * POOL_GROUP]
        pm_ref[:, gi * POOL_GROUP:(gi + 1) * POOL_GROUP] = mixed.astype(BF16)
    hist_ref[0:POOL_HALO, :] = hist_ref[tt:tt + POOL_HALO, :]


def _inproj(x, wqkv, wssm, wpool, cos, sa, sb, pool_w, pool_scale, *, tt):
    bsz, seq, d = x.shape
    row = lambda w: pl.BlockSpec((None, tt, w), lambda b, t: (b, t, 0))
    tab = pl.BlockSpec((tt, LANE), lambda b, t: (t, 0))
    return pl.pallas_call(
        functools.partial(_inproj_kernel, tt=tt),
        grid=(bsz, seq // tt),
        in_specs=[row(d), _const_spec(wqkv.shape), _const_spec(wssm.shape), _const_spec(wpool.shape),
                  tab, tab, tab, _const_spec(pool_w.shape), _const_spec(pool_scale.shape)],
        out_specs=[row(Q_WIDTH), row(KV_WIDTH), row(KV_WIDTH), row(SSM_WIDTH), row(POOL_WIDTH)],
        out_shape=[jax.ShapeDtypeStruct((bsz, seq, Q_WIDTH), BF16),
                   jax.ShapeDtypeStruct((bsz, seq, KV_WIDTH), BF16),
                   jax.ShapeDtypeStruct((bsz, seq, KV_WIDTH), BF16),
                   jax.ShapeDtypeStruct((bsz, seq, SSM_WIDTH), F32),
                   jax.ShapeDtypeStruct((bsz, seq, POOL_WIDTH), BF16)],
        scratch_shapes=[pltpu.VMEM((tt + POOL_HALO, POOL_WIDTH), F32)],
        compiler_params=_params(("arbitrary", "arbitrary")),
        name="inproj",
    )(x, wqkv, wssm, wpool, cos, sa, sb, pool_w, pool_scale)


def _half_lanes(a, b, lo):
    return jnp.where(lo, a, b)


def _swa_kernel(sink_ref, q_ref, k_ref, kp_ref, v_ref, vp_ref, o_ref, *, nsub):
    ti = pl.program_id(1)
    blk = WINDOW
    lo = lax.broadcasted_iota(jnp.int32, (2 * blk, LANE), 1) < HEAD_DIM
    qi = lax.broadcasted_iota(jnp.int32, (2 * blk, 2 * blk), 0) % blk
    si = lax.broadcasted_iota(jnp.int32, (2 * blk, 2 * blk), 1)
    rel = qi + blk - si
    band = (rel >= 0) & (rel < WINDOW)
    top = lax.broadcasted_iota(jnp.int32, (2 * blk, 1), 0) < blk
    zero = jnp.zeros((2 * blk, LANE), F32)

    for j in range(nsub):
        if j == 0:
            kc = jnp.concatenate([kp_ref[...], k_ref[0:blk, :]], axis=0)
            vc = jnp.concatenate([vp_ref[...], v_ref[0:blk, :]], axis=0)
            mask = band & ((ti > 0) | (si >= blk))
        else:
            kc = k_ref[(j - 1) * blk:(j + 1) * blk, :]
            vc = v_ref[(j - 1) * blk:(j + 1) * blk, :]
            mask = band
        kc = kc.astype(F32)
        vc = vc.astype(F32)
        kr = pltpu.roll(kc, HEAD_DIM, 1)
        vr = pltpu.roll(vc, HEAD_DIM, 1)
        kmat = [[_half_lanes(kc, zero, lo), _half_lanes(zero, kr, lo)],
                [_half_lanes(kr, zero, lo), _half_lanes(zero, kc, lo)]]
        vmat = [[_half_lanes(vc, zero, lo), _half_lanes(zero, vr, lo)],
                [_half_lanes(vr, zero, lo), _half_lanes(zero, vc, lo)]]
        rows = pl.ds(j * blk, blk)
        for kh in range(N_KV_HEADS):
            base = kh * 2 * LANE
            qq = jnp.concatenate([q_ref[rows, base:base + LANE], q_ref[rows, base + LANE:base + 2 * LANE]], axis=0)
            probs = []
            for par in range(2):
                s = _dot_nt(qq, kmat[kh][par].astype(BF16))
                s = jnp.where(mask, s, NEG_INF)
                h_top = kh * 4 + par
                sink = jnp.where(top, sink_ref[h_top], sink_ref[h_top + 2])
                m = jnp.maximum(jnp.max(s, axis=-1, keepdims=True), sink)
                p = jnp.exp(s - m)
                p = p / (jnp.sum(p, axis=-1, keepdims=True) + jnp.exp(sink - m))
                probs.append(p.astype(BF16))
            pcat = jnp.concatenate(probs, axis=1)
            vcat = jnp.concatenate([vmat[kh][0], vmat[kh][1]], axis=0).astype(BF16)
            o = _dot(pcat, vcat)
            o_ref[rows, base:base + LANE] = o[0:blk].astype(BF16)
            o_ref[rows, base + LANE:base + 2 * LANE] = o[blk:2 * blk].astype(BF16)


def _swa(q, k, v, sinks, *, tq):
    bsz, seq, _ = q.shape
    nsub = tq // WINDOW
    cur = lambda w: pl.BlockSpec((None, tq, w), lambda b, t: (b, t, 0))
    prev = pl.BlockSpec((None, WINDOW, KV_WIDTH), lambda b, t: (b, jnp.maximum(t * nsub - 1, 0), 0))
    return pl.pallas_call(
        functools.partial(_swa_kernel, nsub=nsub),
        grid=(bsz, seq // tq),
        in_specs=[pl.BlockSpec(memory_space=pltpu.SMEM), cur(Q_WIDTH), cur(KV_WIDTH), prev, cur(KV_WIDTH), prev],
        out_specs=cur(Q_WIDTH),
        out_shape=jax.ShapeDtypeStruct((bsz, seq, Q_WIDTH), BF16),
        compiler_params=_params(("parallel", "parallel")),
        name="swa",
    )(sinks, q, k, k, v, v)


SSM_CHUNKS = 4
SSM_CHUNK_LANES = SSM_LANES // SSM_CHUNKS
SSM_CHUNK_IN = SSM_WIDTH // SSM_CHUNKS
SCAN_LANES = 512


def _ssm_kernel(u_ref, bw_ref, cw_ref, are_ref, aim_ref, d_ref, wglu_ref, o_ref, xs_ref, st_ref, h_ref,
                *, tc, bsz):
    @pl.when(pl.program_id(0) == 0)
    def _():
        h_ref[...] = jnp.zeros(h_ref.shape, F32)

    for b in range(bsz):
        for c in range(SSM_CHUNKS):
            xs_ref[c, pl.ds(b, tc, stride=bsz), :] = u_ref[b, :, c * SSM_CHUNK_IN:(c + 1) * SSM_CHUNK_IN]

    for part in range(2):
        for c in range(SSM_CHUNKS):
            col = part * SSM_LANES + c * SSM_CHUNK_LANES
            st_ref[:, col:col + SSM_CHUNK_LANES] = _dot(xs_ref[c].astype(BF16), bw_ref[part * SSM_CHUNKS + c])

    for jb in range(SSM_LANES // SCAN_LANES):
        re = pl.ds(jb * SCAN_LANES, SCAN_LANES)
        im = pl.ds(SSM_LANES + jb * SCAN_LANES, SCAN_LANES)
        ar = are_ref[:, re]
        ai = aim_ref[:, re]

        def step(t, carry):
            hr, hi = carry
            r = pl.ds(pl.multiple_of(t * bsz, bsz), bsz)
            nr = ar * hr - ai * hi + st_ref[r, re]
            ni = ar * hi + ai * hr + st_ref[r, im]
            st_ref[r, re] = nr
            st_ref[r, im] = ni
            return nr, ni

        hr, hi = lax.fori_loop(0, tc, step, (h_ref[:, re], h_ref[:, im]), unroll=8)
        h_ref[:, re] = hr
        h_ref[:, im] = hi

    ys = []
    for c in range(SSM_CHUNKS):
        sre = st_ref[:, c * SSM_CHUNK_LANES:(c + 1) * SSM_CHUNK_LANES].astype(BF16)
        sim = st_ref[:, SSM_LANES + c * SSM_CHUNK_LANES:SSM_LANES + (c + 1) * SSM_CHUNK_LANES].astype(BF16)
        cols = slice(c * SSM_CHUNK_IN, (c + 1) * SSM_CHUNK_IN)
        y = _dot(sre, cw_ref[c]) + _dot(sim, cw_ref[SSM_CHUNKS + c])
        ys.append(y + d_ref[:, cols] * xs_ref[c])

    hh = jax.nn.gelu(jnp.concatenate(ys, axis=1)).astype(BF16)
    z = _dot(hh, wglu_ref[...])
    res = z[:, :SSM_WIDTH] * jax.nn.sigmoid(z[:, SSM_WIDTH:])
    for c in range(SSM_CHUNKS):
        xs_ref[c] = res[:, c * SSM_CHUNK_IN:(c + 1) * SSM_CHUNK_IN]
    for b in range(bsz):
        for c in range(SSM_CHUNKS):
            o_ref[b, :, c * SSM_CHUNK_IN:(c + 1) * SSM_CHUNK_IN] = (
                xs_ref[c, pl.ds(b, tc, stride=bsz), :].astype(BF16))


def _ssm(u, bw, cw, are, aim, d_skip, wglu, *, tc):
    bsz, seq, _ = u.shape
    blk = pl.BlockSpec((bsz, tc, SSM_WIDTH), lambda t: (0, t, 0))
    return pl.pallas_call(
        functools.partial(_ssm_kernel, tc=tc, bsz=bsz),
        grid=(seq // tc,),
        in_specs=[blk, _const_spec(bw.shape), _const_spec(cw.shape), _const_spec(are.shape),
                  _const_spec(aim.shape), _const_spec(d_skip.shape), _const_spec(wglu.shape)],
        out_specs=blk,
        out_shape=jax.ShapeDtypeStruct((bsz, seq, SSM_WIDTH), BF16),
        scratch_shapes=[pltpu.VMEM((SSM_CHUNKS, tc * bsz, SSM_CHUNK_IN), F32),
                        pltpu.VMEM((tc * bsz, 2 * SSM_LANES), F32),
                        pltpu.VMEM((bsz, 2 * SSM_LANES), F32)],
        compiler_params=_params(("arbitrary",)),
        name="ssm",
    )(u, bw, cw, are, aim, d_skip, wglu)


def _ssm_weights(a_re, a_im, log_dt, b_re, b_im, c_re, c_im, bsz):
    dt = jnp.exp(log_dt)[:, None]
    decay = jnp.exp(dt * a_re)
    abar_re, abar_im = decay * jnp.cos(dt * a_im), decay * jnp.sin(dt * a_im)
    inv_abs2 = 1.0 / (a_re * a_re + a_im * a_im)
    num_re, num_im = abar_re - 1.0, abar_im
    f_re = (num_re * a_re + num_im * a_im) * inv_abs2
    f_im = (num_im * a_re - num_re * a_im) * inv_abs2
    bbar_re = f_re[..., None] * b_re - f_im[..., None] * b_im
    bbar_im = f_re[..., None] * b_im + f_im[..., None] * b_re
    gpc = SSM_GROUPS // SSM_CHUNKS
    eye = jnp.eye(gpc, dtype=F32)

    def in_chunks(bb):
        t = bb.reshape(SSM_CHUNKS, gpc, SSM_STATE, SSM_GROUP)
        return jnp.einsum('cgnp,gh->cgphn', t, eye).reshape(SSM_CHUNKS, gpc * SSM_GROUP, gpc * SSM_STATE)

    def out_chunks(cc):
        t = cc.reshape(SSM_CHUNKS, gpc, SSM_GROUP, SSM_STATE)
        return jnp.einsum('cgpn,gh->cgnhp', t, eye).reshape(SSM_CHUNKS, gpc * SSM_STATE, gpc * SSM_GROUP)

    bw = jnp.concatenate([in_chunks(bbar_re), in_chunks(bbar_im)], axis=0).astype(BF16)
    cw = jnp.concatenate([out_chunks(c_re), out_chunks(-c_im)], axis=0).astype(BF16)
    are = jnp.broadcast_to(abar_re.reshape(1, SSM_LANES), (bsz, SSM_LANES))
    aim = jnp.broadcast_to(abar_im.reshape(1, SSM_LANES), (bsz, SSM_LANES))
    return bw, cw, are, aim


def _merge_kernel(x_ref, a_ref, s_ref, p_ref, wg_ref, bg_ref, wa_ref, ws_ref, wp_ref, wo_ref, g_ref, b_ref, o_ref):
    x = x_ref[...]
    xb = x.astype(BF16)
    merged = None
    for i, (br_ref, w_ref) in enumerate(((a_ref, wa_ref), (s_ref, ws_ref), (p_ref, wp_ref))):
        cols = slice(i * D_MODEL, (i + 1) * D_MODEL)
        gate = jax.nn.sigmoid(_dot(xb, wg_ref[:, cols]) + bg_ref[:, cols])
        term = gate * _dot(br_ref[...], w_ref[...])
        merged = term if merged is None else merged + term
    h = _dot(merged.astype(BF16), wo_ref[...])
    o_ref[...] = _layer_norm(DN_ALPHA * x + h, g_ref[...], b_ref[...])


def _merge(x, a, s, p, wg, bg, wa, ws, wp, wo, g, b, *, tm):
    n, d = x.shape
    row = lambda w: pl.BlockSpec((tm, w), lambda i: (i, 0))
    consts = (wg, bg, wa, ws, wp, wo, g, b)
    return pl.pallas_call(
        _merge_kernel,
        grid=(n // tm,),
        in_specs=[row(d), row(Q_WIDTH), row(SSM_WIDTH), row(POOL_WIDTH)] + [_const_spec(c.shape) for c in consts],
        out_specs=row(d),
        out_shape=jax.ShapeDtypeStruct((n, d), F32),
        compiler_params=_params(("parallel",)),
        name="merge",
    )(x, a, s, p, *consts)


def _matmul_kernel(a_ref, w_ref, o_ref):
    o_ref[...] = _dot(a_ref[...].astype(BF16), w_ref[...]).astype(o_ref.dtype)


def _matmul(a, w, *, tm, out_dtype):
    m, k = a.shape
    n = w.shape[1]
    return pl.pallas_call(
        _matmul_kernel,
        grid=(m // tm,),
        in_specs=[pl.BlockSpec((tm, k), lambda i: (i, 0)), _const_spec(w.shape)],
        out_specs=pl.BlockSpec((tm, n), lambda i: (i, 0)),
        out_shape=jax.ShapeDtypeStruct((m, n), out_dtype),
        compiler_params=_params(("parallel",)),
        name="kvproj",
    )(a, w)


def _xattn_kernel(x_ref, k_ref, v_ref, wq_ref, wo_ref, g_ref, b_ref, o_ref):
    x = x_ref[...]
    q = (_dot(x.astype(BF16), wq_ref[...]) * (X_HEAD_DIM ** -0.5)).astype(BF16)
    outs = []
    for h in range(X_HEADS):
        cols = slice(h * X_HEAD_DIM, (h + 1) * X_HEAD_DIM)
        s = _dot_nt(q[:, cols], k_ref[:, cols])
        m = jnp.max(s, axis=-1, keepdims=True)
        p = jnp.exp(s - m)
        p = p / jnp.sum(p, axis=-1, keepdims=True)
        outs.append(_dot(p.astype(BF16), v_ref[:, cols]).astype(BF16))
    c = _dot(jnp.concatenate(outs, axis=1), wo_ref[...])
    o_ref[...] = _layer_norm(DN_ALPHA * x + c, g_ref[...], b_ref[...])


def _xattn(x, kv, wq, wo, g, b, *, tm):
    bsz, seq, d = x.shape
    nmem = kv.shape[1]
    row = pl.BlockSpec((None, tm, d), lambda bi, t: (bi, t, 0))
    kspec = pl.BlockSpec((None, nmem, d), lambda bi, t: (bi, 0, 0))
    vspec = pl.BlockSpec((None, nmem, d), lambda bi, t: (bi, 0, 1))
    consts = (wq, wo, g, b)
    return pl.pallas_call(
        _xattn_kernel,
        grid=(bsz, seq // tm),
        in_specs=[row, kspec, vspec] + [_const_spec(c.shape) for c in consts],
        out_specs=row,
        out_shape=jax.ShapeDtypeStruct((bsz, seq, d), F32),
        compiler_params=_params(("parallel", "parallel")),
        name="xattn",
    )(x, kv, kv, *consts)


def _ffn_kernel(x_ref, wg_ref, wu_ref, wd_ref, g_ref, b_ref, o_ref, *, nchunk):
    x = x_ref[...]
    xb = x.astype(BF16)
    fc = wg_ref.shape[1] // nchunk
    acc = None
    for c in range(nchunk):
        cols = slice(c * fc, (c + 1) * fc)
        hid = jax.nn.silu(_dot(xb, wg_ref[:, cols])) * _dot(xb, wu_ref[:, cols])
        part = _dot(hid.astype(BF16), wd_ref[cols, :])
        acc = part if acc is None else acc + part
    o_ref[...] = _layer_norm(DN_ALPHA * x + acc, g_ref[...], b_ref[...])


def _ffn(x, wg, wu, wd, g, b, *, tm, nchunk):
    n, d = x.shape
    row = pl.BlockSpec((tm, d), lambda i: (i, 0))
    consts = (wg, wu, wd, g, b)
    return pl.pallas_call(
        functools.partial(_ffn_kernel, nchunk=nchunk),
        grid=(n // tm,),
        in_specs=[row] + [_const_spec(c.shape) for c in consts],
        out_specs=row,
        out_shape=jax.ShapeDtypeStruct((n, d), F32),
        compiler_params=_params(("parallel",)),
        name="ffn",
    )(x, *consts)


MOE_TT = 512
SEG_ALIGN = 16
MOE_RB = 512
MOE_RTILE = 2 * MOE_TT + N_EXPERTS * SEG_ALIGN
MOE_FF_CHUNKS = 4
R_E1, R_E2, R_W1, R_W2, R_K1, R_K2 = range(6)


def _router_kernel(x_ref, wr_ref, br_ref, route_ref, routet_ref, cnt_ref, *, tm):
    logits = lax.dot_general(wr_ref[...], x_ref[...], (((1,), (1,)), ((), ())),
                             precision=lax.Precision.HIGHEST, preferred_element_type=F32) + br_ref[...]
    eid = lax.broadcasted_iota(jnp.int32, logits.shape, 0)
    v1 = jnp.max(logits, axis=0, keepdims=True)
    e1 = jnp.min(jnp.where(logits == v1, eid, N_EXPERTS), axis=0, keepdims=True)
    rest = jnp.where(eid == e1, -jnp.inf, logits)
    v2 = jnp.max(rest, axis=0, keepdims=True)
    e2 = jnp.min(jnp.where(rest == v2, eid, N_EXPERTS), axis=0, keepdims=True)
    t = jnp.exp(v2 - v1)
    w1 = 1.0 / (1.0 + t)
    w2 = t / (1.0 + t)
    sel1, sel2 = eid == e1, eid == e2
    member = (sel1 | sel2).astype(BF16)
    before = (lax.broadcasted_iota(jnp.int32, (tm, tm), 0) < lax.broadcasted_iota(jnp.int32, (tm, tm), 1))
    rank = _dot(member, before.astype(BF16))
    k1 = jnp.sum(jnp.where(sel1, rank, 0.0), axis=0, keepdims=True)
    k2 = jnp.sum(jnp.where(sel2, rank, 0.0), axis=0, keepdims=True)
    route = jnp.zeros(logits.shape, F32)
    for r, val in ((R_E1, e1.astype(F32)), (R_E2, e2.astype(F32)), (R_W1, w1), (R_W2, w2), (R_K1, k1), (R_K2, k2)):
        route = jnp.where(eid == r, val, route)
    route_ref[...] = route
    padded = jnp.concatenate([route, jnp.zeros((LANE - N_EXPERTS, tm), F32)], axis=0)
    routet_ref[...] = padded.T
    cnt = jnp.sum(member.astype(F32), axis=1, keepdims=True)
    cnt_ref[...] = jnp.broadcast_to(cnt, (N_EXPERTS, LANE))


def _router(x, wr_t, br, *, tm):
    n, d = x.shape
    nt = n // tm
    return pl.pallas_call(
        functools.partial(_router_kernel, tm=tm),
        grid=(nt,),
        in_specs=[pl.BlockSpec((tm, d), lambda i: (i, 0)), _const_spec(wr_t.shape), _const_spec(br.shape)],
        out_specs=[pl.BlockSpec((N_EXPERTS, tm), lambda i: (0, i)),
                   pl.BlockSpec((tm, LANE), lambda i: (i, 0)),
                   pl.BlockSpec((None, N_EXPERTS, LANE), lambda i: (i, 0, 0))],
        out_shape=[jax.ShapeDtypeStruct((N_EXPERTS, n), F32),
                   jax.ShapeDtypeStruct((n, LANE), F32),
                   jax.ShapeDtypeStruct((nt, N_EXPERTS, LANE), F32)],
        compiler_params=_params(("parallel",)),
        name="router",
    )(x, wr_t, br)


def _segment_copies(pc_ref, off_ref, tile, make_copy, sem_ref):
    loc = jnp.int32(0)
    total = jnp.int32(0)
    for e in range(N_EXPERTS):
        npieces = pc_ref[tile * N_EXPERTS + e] // SEG_ALIGN
        glob = off_ref[tile * N_EXPERTS + e]

        def start(p, carry, loc=loc, glob=glob):
            make_copy(pl.multiple_of(loc + p * SEG_ALIGN, SEG_ALIGN),
                      pl.multiple_of(glob + p * SEG_ALIGN, SEG_ALIGN)).start()
            return carry

        lax.fori_loop(0, npieces, start, 0)
        loc = loc + npieces * SEG_ALIGN
        total = total + npieces

    def wait(p, carry):
        make_copy(0, 0).wait()
        return carry

    lax.fori_loop(0, total, wait, 0)


def _tile_offsets(pc_ref, tile):
    locs, acc = [], jnp.int32(0)
    for e in range(N_EXPERTS):
        locs.append(acc)
        acc = acc + pc_ref[tile * N_EXPERTS + e]
    return locs


def _dest_rows(expert, rank, locs):
    base = jnp.zeros(expert.shape, F32)
    for e in range(N_EXPERTS):
        base = jnp.where(expert == float(e), locs[e].astype(F32), base)
    return (base + rank).astype(jnp.int32)


def _dispatch_kernel(pc_ref, off_ref, x_ref, route_ref, buf_in_ref, buf_ref, stage_ref, sem_ref):
    del buf_in_ref
    i = pl.program_id(0)
    locs = _tile_offsets(pc_ref, i)
    d1 = _dest_rows(route_ref[R_E1:R_E1 + 1, :], route_ref[R_K1:R_K1 + 1, :], locs)
    d2 = _dest_rows(route_ref[R_E2:R_E2 + 1, :], route_ref[R_K2:R_K2 + 1, :], locs)
    rows = lax.broadcasted_iota(jnp.int32, (MOE_RTILE, MOE_TT), 0)
    onehot = ((rows == d1) | (rows == d2)).astype(BF16)
    stage_ref[...] = _dot(onehot, x_ref[...].astype(BF16)).astype(BF16)

    def make_copy(loc, glob):
        return pltpu.make_async_copy(stage_ref.at[pl.ds(loc, SEG_ALIGN), :],
                                     buf_ref.at[pl.ds(glob, SEG_ALIGN), :], sem_ref.at[0])

    _segment_copies(pc_ref, off_ref, i, make_copy, sem_ref)


def _dispatch(x, route, pc, off, buf):
    n, d = x.shape
    return pl.pallas_call(
        _dispatch_kernel,
        grid_spec=pltpu.PrefetchScalarGridSpec(
            num_scalar_prefetch=2,
            grid=(n // MOE_TT,),
            in_specs=[pl.BlockSpec((MOE_TT, d), lambda i, *_: (i, 0)),
                      pl.BlockSpec((N_EXPERTS, MOE_TT), lambda i, *_: (0, i)),
                      pl.BlockSpec(memory_space=pl.ANY)],
            out_specs=pl.BlockSpec(memory_space=pl.ANY),
            scratch_shapes=[pltpu.VMEM((MOE_RTILE, d), BF16), pltpu.SemaphoreType.DMA((1,))]),
        out_shape=jax.ShapeDtypeStruct(buf.shape, BF16),
        input_output_aliases={4: 0},
        compiler_params=_params(("arbitrary",)),
        name="dispatch",
    )(pc, off, x, route, buf)


def _experts_kernel(be_ref, nused_ref, x_ref, wg_ref, wu_ref, wd_ref, o_ref):
    @pl.when(pl.program_id(0) < nused_ref[0])
    def _():
        xb = x_ref[...]
        fc = wg_ref.shape[1] // MOE_FF_CHUNKS
        acc = None
        for c in range(MOE_FF_CHUNKS):
            cols = slice(c * fc, (c + 1) * fc)
            hid = jax.nn.silu(_dot(xb, wg_ref[:, cols])) * _dot(xb, wu_ref[:, cols])
            part = _dot(hid.astype(BF16), wd_ref[cols, :])
            acc = part if acc is None else acc + part
        o_ref[...] = acc.astype(BF16)

    @pl.when(pl.program_id(0) >= nused_ref[0])
    def _():
        o_ref[...] = jnp.zeros(o_ref.shape, BF16)


def _experts(buf, be, nused, wgu, wd):
    rmax, d = buf.shape
    ff = wd.shape[1]
    blk = pl.BlockSpec((MOE_RB, d), lambda i, be, nu: (jnp.minimum(i, nu[0] - 1), 0))
    out_blk = pl.BlockSpec((MOE_RB, d), lambda i, be, nu: (i, 0))
    once = pl.Buffered(1)
    return pl.pallas_call(
        _experts_kernel,
        grid_spec=pltpu.PrefetchScalarGridSpec(
            num_scalar_prefetch=2,
            grid=(rmax // MOE_RB,),
            in_specs=[blk,
                      pl.BlockSpec((None, d, ff), lambda i, be, nu: (be[i], 0, 0), pipeline_mode=once),
                      pl.BlockSpec((None, d, ff), lambda i, be, nu: (be[i], 0, 1), pipeline_mode=once),
                      pl.BlockSpec((None, ff, d), lambda i, be, nu: (be[i], 0, 0), pipeline_mode=once)],
            out_specs=out_blk),
        out_shape=jax.ShapeDtypeStruct((rmax, d), BF16),
        compiler_params=_params(("arbitrary",)),
        name="experts",
    )(be, nused, buf, wgu, wgu, wd)


def _combine_kernel(pc_ref, off_ref, x_ref, routet_ref, g_ref, b_ref, buf_ref, o_ref, stage_ref, sem_ref):
    i = pl.program_id(0)

    @pl.when(i == 0)
    def _():
        stage_ref[...] = jnp.zeros(stage_ref.shape, BF16)

    def make_copy(loc, glob):
        return pltpu.make_async_copy(buf_ref.at[pl.ds(glob, SEG_ALIGN), :],
                                     stage_ref.at[pl.ds(loc, SEG_ALIGN), :], sem_ref.at[0])

    _segment_copies(pc_ref, off_ref, i, make_copy, sem_ref)

    locs = _tile_offsets(pc_ref, i)
    rt = routet_ref[...]
    col = lambda r: rt[:, r:r + 1]
    d1 = _dest_rows(col(R_E1), col(R_K1), locs)
    d2 = _dest_rows(col(R_E2), col(R_K2), locs)
    lanes = lax.broadcasted_iota(jnp.int32, (MOE_TT, MOE_RTILE), 1)
    y = stage_ref[...]
    f = col(R_W1) * _dot((lanes == d1).astype(BF16), y) + col(R_W2) * _dot((lanes == d2).astype(BF16), y)
    o_ref[...] = _layer_norm(DN_ALPHA * x_ref[...] + f, g_ref[...], b_ref[...])


def _combine(x, routet, pc, off, buf, g, b):
    n, d = x.shape
    row = pl.BlockSpec((MOE_TT, d), lambda i, *_: (i, 0))
    return pl.pallas_call(
        _combine_kernel,
        grid_spec=pltpu.PrefetchScalarGridSpec(
            num_scalar_prefetch=2,
            grid=(n // MOE_TT,),
            in_specs=[row, pl.BlockSpec((MOE_TT, LANE), lambda i, *_: (i, 0)),
                      pl.BlockSpec(g.shape, lambda i, *_: (0, 0)), pl.BlockSpec(b.shape, lambda i, *_: (0, 0)),
                      pl.BlockSpec(memory_space=pl.ANY)],
            out_specs=row,
            scratch_shapes=[pltpu.VMEM((MOE_RTILE, d), BF16), pltpu.SemaphoreType.DMA((1,))]),
        out_shape=jax.ShapeDtypeStruct((n, d), F32),
        compiler_params=_params(("arbitrary",)),
        name="combine",
    )(pc, off, x, routet, g, b, buf)


def _moe(x, w_router, b_router, w_gu, w_down, g, b):
    n, d = x.shape
    nt = n // MOE_TT
    route, routet, cnt = _router(x, w_router.T, b_router.reshape(N_EXPERTS, 1), tm=MOE_TT)

    cnt = cnt[:, :, 0].astype(jnp.int32)
    pc = (cnt + SEG_ALIGN - 1) // SEG_ALIGN * SEG_ALIGN
    region = (jnp.sum(pc, axis=0) + MOE_RB - 1) // MOE_RB * MOE_RB
    base = jnp.cumsum(region) - region
    off = base[None, :] + jnp.cumsum(pc, axis=0) - pc
    rmax = (2 * n + nt * N_EXPERTS * (SEG_ALIGN - 1) + N_EXPERTS * (MOE_RB - 1) + MOE_RB - 1) // MOE_RB * MOE_RB
    blk_end = jnp.cumsum(region // MOE_RB)
    nused = blk_end[-1:]
    blk = jnp.minimum(jnp.arange(rmax // MOE_RB, dtype=jnp.int32), nused - 1)
    be = jnp.sum(blk[:, None] >= blk_end[None, :], axis=1).astype(jnp.int32)
    pc, off = pc.reshape(-1), off.reshape(-1).astype(jnp.int32)

    buf = _dispatch(x, route, pc, off, jnp.zeros((rmax, d), BF16))
    y = _experts(buf, be, nused.astype(jnp.int32), w_gu.astype(BF16), w_down.astype(BF16))
    return _combine(x, routet, pc, off, y, g, b)


def _rotary_tables(seq):
    half = ROT_DIM // 2
    inv_freq = jnp.power(jnp.float32(ROPE_THETA), -jnp.arange(half, dtype=F32) / half)
    ang = jnp.arange(seq, dtype=jnp.int32).astype(F32)[:, None] * inv_freq[None, :]
    cos, sin = jnp.cos(ang), jnp.sin(ang)
    d = jnp.arange(LANE) % HEAD_DIM
    first, second = d < half, (d >= half) & (d < ROT_DIM)
    idx = d % half
    cos_t = jnp.where((first | second)[None, :], cos[:, idx], 1.0)
    sin_a = jnp.where(first[None, :], -sin[:, idx], 0.0)
    sin_b = jnp.where(second[None, :], sin[:, idx], 0.0)
    return cos_t, sin_a, sin_b


def kernel(x, mem, w_in, b_gate, attn_sinks, ssm_a_re, ssm_a_im, ssm_log_dt, ssm_b_re, ssm_b_im, ssm_c_re, ssm_c_im, ssm_d, ssm_w_glu, pool_w, pool_scale, w_br_attn, w_br_ssm, w_br_pool, w_o, ln1_g, ln1_b, xa_wq, xa_wkv, xa_wo, ln2_g, ln2_b, ffn_w_gu, ffn_w_down, moe_w_router, moe_b_router, moe_w_gu, moe_w_down, ln3_g, ln3_b):
    bsz, seq, d = x.shape
    n = bsz * seq
    nmem = mem.shape[1]
    cos_t, sin_a, sin_b = _rotary_tables(seq)
    row2 = lambda v: v.reshape(1, -1)
    qkv_w = Q_WIDTH + 2 * KV_WIDTH
    mem2 = mem.reshape(bsz * nmem, d)

    for i in range(DEPTH):
        wi = w_in[i].astype(BF16)
        wqkv, wssm = wi[:, :qkv_w], wi[:, qkv_w:qkv_w + SSM_WIDTH]
        wpool = wi[:, qkv_w + SSM_WIDTH:qkv_w + SSM_WIDTH + POOL_WIDTH]
        wgate = wi[:, qkv_w + SSM_WIDTH + POOL_WIDTH:]
        q, k, v, u, pm = _inproj(x, wqkv, wssm, wpool, cos_t, sin_a, sin_b, pool_w[i].astype(BF16),
                                 row2(pool_scale[i]), tt=512)
        attn = _swa(q, k, v, attn_sinks[i], tq=512)
        bw, cw, are, aim = _ssm_weights(ssm_a_re[i], ssm_a_im[i], ssm_log_dt[i], ssm_b_re[i], ssm_b_im[i],
                                        ssm_c_re[i], ssm_c_im[i], bsz)
        hs = _ssm(u, bw, cw, are, aim, row2(ssm_d[i]), ssm_w_glu[i].astype(BF16), tc=128)
        x2 = _merge(x.reshape(n, d), attn.reshape(n, -1), hs.reshape(n, -1), pm.reshape(n, -1),
                    wgate, row2(b_gate[i]), w_br_attn[i].astype(BF16), w_br_ssm[i].astype(BF16),
                    w_br_pool[i].astype(BF16), w_o[i].astype(BF16), row2(ln1_g[i]), row2(ln1_b[i]), tm=512)

        kv = _matmul(mem2, xa_wkv[i].astype(BF16), tm=512, out_dtype=BF16).reshape(bsz, nmem, 2 * d)
        x3 = _xattn(x2.reshape(bsz, seq, d), kv, xa_wq[i].astype(BF16), xa_wo[i].astype(BF16),
                    row2(ln2_g[i]), row2(ln2_b[i]), tm=512).reshape(n, d)

        j = i // 2
        if i % 2 == 0:
            wgu = ffn_w_gu[j].astype(BF16)
            x4 = _ffn(x3, wgu[:, :D_FF], wgu[:, D_FF:], ffn_w_down[j].astype(BF16),
                      row2(ln3_g[i]), row2(ln3_b[i]), tm=512, nchunk=2)
        else:
            x4 = _moe(x3, moe_w_router[j], moe_b_router[j], moe_w_gu[j], moe_w_down[j],
                      row2(ln3_g[i]), row2(ln3_b[i]))
        x = x4.reshape(bsz, seq, d)
    return x
```

```python
import functools

import jax
import jax.numpy as jnp
from jax import lax
from jax.experimental import pallas as pl
from jax.experimental.pallas import tpu as pltpu

D_MODEL = 1024
DEPTH = 4
N_Q_HEADS = 8
N_KV_HEADS = 2
HEAD_DIM = 64
WINDOW = 128
ROT_DIM = HEAD_DIM // 4
ROPE_THETA = 500000.0
SSM_WIDTH = D_MODEL // 2
SSM_GROUP = 16
SSM_GROUPS = SSM_WIDTH // SSM_GROUP
SSM_STATE = 64
SSM_LANES = SSM_GROUPS * SSM_STATE
POOL_WINDOWS = (2, 4, 8, 16)
POOL_WIDTH = D_MODEL // 2
POOL_GROUP = POOL_WIDTH // len(POOL_WINDOWS)
POOL_HALO = 16
X_HEADS = 4
X_HEAD_DIM = D_MODEL // X_HEADS
D_FF = 2816
N_EXPERTS = 8
D_FF_EXPERT = 3584
Q_WIDTH = N_Q_HEADS * HEAD_DIM
KV_WIDTH = N_KV_HEADS * HEAD_DIM
QKV_WIDTH = Q_WIDTH + 2 * KV_WIDTH
BRANCH_IN_WIDTH = QKV_WIDTH + SSM_WIDTH + POOL_WIDTH
DN_ALPHA = (2.0 * DEPTH) ** 0.25
LN_EPS = 1e-5
NEG_INF = -1e30

TT_INPROJ = 512
TQ_SWA = 512
TC_SSM = 128
TM_MERGE = 1024
TM_XATTN = 512
TM_FFN = 1024
FFN_CHUNKS = 2

LANE = 128
MXU_TILE = 256
VMEM_LIMIT = 56 * 1024 * 1024

BF16 = jnp.bfloat16
F32 = jnp.float32


def _dot(a, b):
    return jnp.dot(a, b, preferred_element_type=F32)


def _dot_nt(a, b):
    return lax.dot_general(a, b, (((1,), (1,)), ((), ())), preferred_element_type=F32)


def _layer_norm(y, g, b):
    mu = jnp.mean(y, axis=-1, keepdims=True)
    d = y - mu
    var = jnp.mean(d * d, axis=-1, keepdims=True)
    return d * lax.rsqrt(var + LN_EPS) * g + b


def _const_spec(shape):
    nd = len(shape)
    return pl.BlockSpec(shape, lambda *_: (0,) * nd, pipeline_mode=pl.Buffered(1))


def _layer_spec(stack, layer, width=None, col=0):
    rest = stack.shape[1:]
    block = (None,) + rest[:-1] + (rest[-1] if width is None else width,)
    index = (layer,) + (0,) * (len(rest) - 1) + (col,)
    return pl.BlockSpec(block, lambda *_: index, pipeline_mode=pl.Buffered(1))


def _params(sem, limit=VMEM_LIMIT):
    return pltpu.CompilerParams(dimension_semantics=sem, vmem_limit_bytes=limit)


def _rotary(t, cos, sin_a, sin_b):
    return t * cos + pltpu.roll(t, LANE - ROT_DIM // 2, 1) * sin_a + pltpu.roll(t, ROT_DIM // 2, 1) * sin_b


def _inproj_kernel(x_ref, w_ref, cos_ref, sa_ref, sb_ref, pw_ref, ps_ref,
                   q_ref, k_ref, v_ref, u_ref, pm_ref, hist_ref, *, tt):
    ti = pl.program_id(1)
    xb = x_ref[...].astype(BF16)
    cos, sa, sb = cos_ref[...], sa_ref[...], sb_ref[...]

    qkv = _dot(xb, w_ref[:, 0:QKV_WIDTH])
    for c in range(Q_WIDTH // LANE):
        t = qkv[:, c * LANE:(c + 1) * LANE] * (HEAD_DIM ** -0.5)
        q_ref[:, c * LANE:(c + 1) * LANE] = _rotary(t, cos, sa, sb).astype(BF16)
    k_ref[...] = _rotary(qkv[:, Q_WIDTH:Q_WIDTH + KV_WIDTH], cos, sa, sb).astype(BF16)
    v_ref[...] = qkv[:, Q_WIDTH + KV_WIDTH:].astype(BF16)

    u_ref[...] = _dot(xb, w_ref[:, QKV_WIDTH:QKV_WIDTH + SSM_WIDTH])

    @pl.when(ti == 0)
    def _():
        hist_ref[0:POOL_HALO, :] = jnp.zeros((POOL_HALO, POOL_WIDTH), F32)

    up = _dot(xb, w_ref[:, QKV_WIDTH + SSM_WIDTH:BRANCH_IN_WIDTH])
    hist_ref[POOL_HALO:POOL_HALO + tt, :] = up
    pos1 = (ti * tt + 1 + lax.broadcasted_iota(jnp.int32, (tt, 1), 0)).astype(F32)
    for gi, w in enumerate(POOL_WINDOWS):
        lanes = slice(gi * POOL_GROUP, (gi + 1) * POOL_GROUP)
        acc = hist_ref[:, lanes]
        k = 1
        while k < w:
            acc = acc + pltpu.roll(acc, k, 0)
            k *= 2
        pooled = acc[POOL_HALO:, :] / jnp.minimum(pos1, float(w)) - up[:, lanes]
        mixed = _dot(pooled.astype(BF16), pw_ref[gi]) * ps_ref[:, lanes]
        pm_ref[:, lanes] = mixed.astype(BF16)
    hist_ref[0:POOL_HALO, :] = hist_ref[tt:tt + POOL_HALO, :]


def _inproj(x, w_in, cos, sa, sb, pool_w, pool_scale, *, layer, tt):
    bsz, seq, d = x.shape
    row = lambda w: pl.BlockSpec((None, tt, w), lambda b, t: (b, t, 0))
    tab = pl.BlockSpec((tt, LANE), lambda b, t: (t, 0))
    return pl.pallas_call(
        functools.partial(_inproj_kernel, tt=tt),
        grid=(bsz, seq // tt),
        in_specs=[row(d), _layer_spec(w_in, layer, width=BRANCH_IN_WIDTH), tab, tab, tab,
                  _layer_spec(pool_w, layer), _layer_spec(pool_scale, layer)],
        out_specs=[row(Q_WIDTH), row(KV_WIDTH), row(KV_WIDTH), row(SSM_WIDTH), row(POOL_WIDTH)],
        out_shape=[jax.ShapeDtypeStruct((bsz, seq, Q_WIDTH), BF16),
                   jax.ShapeDtypeStruct((bsz, seq, KV_WIDTH), BF16),
                   jax.ShapeDtypeStruct((bsz, seq, KV_WIDTH), BF16),
                   jax.ShapeDtypeStruct((bsz, seq, SSM_WIDTH), F32),
                   jax.ShapeDtypeStruct((bsz, seq, POOL_WIDTH), BF16)],
        scratch_shapes=[pltpu.VMEM((tt + POOL_HALO, POOL_WIDTH), F32)],
        compiler_params=_params(("arbitrary", "arbitrary")),
        name="inproj",
    )(x, w_in, cos, sa, sb, pool_w, pool_scale)


def _half_lanes(a, b, lo):
    return jnp.where(lo, a, b)


def _swa_kernel(sink_ref, q_ref, k_ref, kp_ref, v_ref, vp_ref, o_ref, *, nsub):
    ti = pl.program_id(1)
    blk = WINDOW
    lo = lax.broadcasted_iota(jnp.int32, (2 * blk, LANE), 1) < HEAD_DIM
    qi = lax.broadcasted_iota(jnp.int32, (2 * blk, 2 * blk), 0) % blk
    si = lax.broadcasted_iota(jnp.int32, (2 * blk, 2 * blk), 1)
    rel = qi + blk - si
    band = (rel >= 0) & (rel < WINDOW)
    top = lax.broadcasted_iota(jnp.int32, (2 * blk, 1), 0) < blk
    zero = jnp.zeros((2 * blk, LANE), F32)

    for j in range(nsub):
        if j == 0:
            kc = jnp.concatenate([kp_ref[...], k_ref[0:blk, :]], axis=0)
            vc = jnp.concatenate([vp_ref[...], v_ref[0:blk, :]], axis=0)
            mask = band & ((ti > 0) | (si >= blk))
        else:
            kc = k_ref[(j - 1) * blk:(j + 1) * blk, :]
            vc = v_ref[(j - 1) * blk:(j + 1) * blk, :]
            mask = band
        kc = kc.astype(F32)
        vc = vc.astype(F32)
        kr = pltpu.roll(kc, HEAD_DIM, 1)
        vr = pltpu.roll(vc, HEAD_DIM, 1)
        kmat = [[_half_lanes(kc, zero, lo), _half_lanes(zero, kr, lo)],
                [_half_lanes(kr, zero, lo), _half_lanes(zero, kc, lo)]]
        vmat = [[_half_lanes(vc, zero, lo), _half_lanes(zero, vr, lo)],
                [_half_lanes(vr, zero, lo), _half_lanes(zero, vc, lo)]]
        rows = pl.ds(j * blk, blk)
        for kh in range(N_KV_HEADS):
            base = kh * 2 * LANE
            qq = jnp.concatenate([q_ref[rows, base:base + LANE], q_ref[rows, base + LANE:base + 2 * LANE]], axis=0)
            probs, inv = [], []
            for par in range(2):
                s = _dot_nt(qq, kmat[kh][par].astype(BF16))
                s = jnp.where(mask, s, NEG_INF)
                h_top = kh * 4 + par
                sink = jnp.where(top, sink_ref[h_top], sink_ref[h_top + 2])
                m = jnp.maximum(jnp.max(s, axis=-1, keepdims=True), sink)
                p = jnp.exp(s - m)
                inv.append(1.0 / (jnp.sum(p, axis=-1, keepdims=True) + jnp.exp(sink - m)))
                probs.append(p.astype(BF16))
            pcat = jnp.concatenate(probs, axis=1)
            vcat = jnp.concatenate([vmat[kh][0], vmat[kh][1]], axis=0).astype(BF16)
            o = _dot(pcat, vcat) * jnp.where(lo, inv[0], inv[1])
            o_ref[rows, base:base + LANE] = o[0:blk].astype(BF16)
            o_ref[rows, base + LANE:base + 2 * LANE] = o[blk:2 * blk].astype(BF16)


def _swa(q, k, v, sinks, *, layer, tq):
    bsz, seq, _ = q.shape
    nsub = tq // WINDOW
    cur = lambda w: pl.BlockSpec((None, tq, w), lambda b, t: (b, t, 0))
    prev = pl.BlockSpec((None, WINDOW, KV_WIDTH), lambda b, t: (b, jnp.maximum(t * nsub - 1, 0), 0))
    return pl.pallas_call(
        functools.partial(_swa_kernel, nsub=nsub),
        grid=(bsz, seq // tq),
        in_specs=[pl.BlockSpec(memory_space=pltpu.SMEM), cur(Q_WIDTH), cur(KV_WIDTH), prev, cur(KV_WIDTH), prev],
        out_specs=cur(Q_WIDTH),
        out_shape=jax.ShapeDtypeStruct((bsz, seq, Q_WIDTH), BF16),
        compiler_params=_params(("parallel", "parallel")),
        name="swa",
    )(sinks[layer], q, k, k, v, v)


SSM_CHUNKS = 4
SSM_CHUNK_LANES = SSM_LANES // SSM_CHUNKS
SSM_CHUNK_IN = SSM_WIDTH // SSM_CHUNKS
SCAN_LANES = 512


def _ssm_kernel(u_ref, bw_ref, cw_ref, are_ref, aim_ref, d_ref, wglu_ref, o_ref, xs_ref, st_ref, h_ref,
                *, tc, bsz):
    @pl.when(pl.program_id(0) == 0)
    def _():
        h_ref[...] = jnp.zeros(h_ref.shape, F32)

    for b in range(bsz):
        for c in range(SSM_CHUNKS):
            xs_ref[c, pl.ds(b, tc, stride=bsz), :] = u_ref[b, :, c * SSM_CHUNK_IN:(c + 1) * SSM_CHUNK_IN]

    for part in range(2):
        for c in range(SSM_CHUNKS):
            col = part * SSM_LANES + c * SSM_CHUNK_LANES
            st_ref[:, col:col + SSM_CHUNK_LANES] = _dot(xs_ref[c].astype(BF16), bw_ref[part * SSM_CHUNKS + c])

    for jb in range(SSM_LANES // SCAN_LANES):
        re = pl.ds(jb * SCAN_LANES, SCAN_LANES)
        im = pl.ds(SSM_LANES + jb * SCAN_LANES, SCAN_LANES)
        ar = are_ref[:, re]
        ai = aim_ref[:, re]

        def step(t, carry):
            hr, hi = carry
            r = pl.ds(pl.multiple_of(t * bsz, bsz), bsz)
            nr = ar * hr - ai * hi + st_ref[r, re]
            ni = ar * hi + ai * hr + st_ref[r, im]
            st_ref[r, re] = nr
            st_ref[r, im] = ni
            return nr, ni

        hr, hi = lax.fori_loop(0, tc, step, (h_ref[:, re], h_ref[:, im]), unroll=8)
        h_ref[:, re] = hr
        h_ref[:, im] = hi

    ys = []
    for c in range(SSM_CHUNKS):
        sre = st_ref[:, c * SSM_CHUNK_LANES:(c + 1) * SSM_CHUNK_LANES].astype(BF16)
        sim = st_ref[:, SSM_LANES + c * SSM_CHUNK_LANES:SSM_LANES + (c + 1) * SSM_CHUNK_LANES].astype(BF16)
        cols = slice(c * SSM_CHUNK_IN, (c + 1) * SSM_CHUNK_IN)
        y = _dot(sre, cw_ref[c]) + _dot(sim, cw_ref[SSM_CHUNKS + c])
        ys.append(y + d_ref[:, cols] * xs_ref[c])

    hh = jax.nn.gelu(jnp.concatenate(ys, axis=1)).astype(BF16)
    z = _dot(hh, wglu_ref[...])
    res = z[:, :SSM_WIDTH] * jax.nn.sigmoid(z[:, SSM_WIDTH:])
    for c in range(SSM_CHUNKS):
        xs_ref[c] = res[:, c * SSM_CHUNK_IN:(c + 1) * SSM_CHUNK_IN]
    for b in range(bsz):
        for c in range(SSM_CHUNKS):
            o_ref[b, :, c * SSM_CHUNK_IN:(c + 1) * SSM_CHUNK_IN] = (
                xs_ref[c, pl.ds(b, tc, stride=bsz), :].astype(BF16))


def _ssm(u, bw, cw, are, aim, d_skip, wglu, *, layer, tc):
    bsz, seq, _ = u.shape
    blk = pl.BlockSpec((bsz, tc, SSM_WIDTH), lambda t: (0, t, 0))
    return pl.pallas_call(
        functools.partial(_ssm_kernel, tc=tc, bsz=bsz),
        grid=(seq // tc,),
        in_specs=[blk, _const_spec(bw.shape), _const_spec(cw.shape), _const_spec(are.shape),
                  _const_spec(aim.shape), _layer_spec(d_skip, layer), _layer_spec(wglu, layer)],
        out_specs=blk,
        out_shape=jax.ShapeDtypeStruct((bsz, seq, SSM_WIDTH), BF16),
        scratch_shapes=[pltpu.VMEM((SSM_CHUNKS, tc * bsz, SSM_CHUNK_IN), F32),
                        pltpu.VMEM((tc * bsz, 2 * SSM_LANES), F32),
                        pltpu.VMEM((bsz, 2 * SSM_LANES), F32)],
        compiler_params=_params(("arbitrary",)),
        name="ssm",
    )(u, bw, cw, are, aim, d_skip, wglu)


def _ssm_weights(a_re, a_im, log_dt, b_re, b_im, c_re, c_im, bsz):
    dt = jnp.exp(log_dt)[:, None]
    decay = jnp.exp(dt * a_re)
    abar_re, abar_im = decay * jnp.cos(dt * a_im), decay * jnp.sin(dt * a_im)
    inv_abs2 = 1.0 / (a_re * a_re + a_im * a_im)
    num_re, num_im = abar_re - 1.0, abar_im
    f_re = (num_re * a_re + num_im * a_im) * inv_abs2
    f_im = (num_im * a_re - num_re * a_im) * inv_abs2
    bbar_re = f_re[..., None] * b_re - f_im[..., None] * b_im
    bbar_im = f_re[..., None] * b_im + f_im[..., None] * b_re
    gpc = SSM_GROUPS // SSM_CHUNKS
    eye = jnp.eye(gpc, dtype=F32)

    def in_chunks(bb):
        t = bb.reshape(SSM_CHUNKS, gpc, SSM_STATE, SSM_GROUP)
        return jnp.einsum('cgnp,gh->cgphn', t, eye).reshape(SSM_CHUNKS, gpc * SSM_GROUP, gpc * SSM_STATE)

    def out_chunks(cc):
        t = cc.reshape(SSM_CHUNKS, gpc, SSM_GROUP, SSM_STATE)
        return jnp.einsum('cgpn,gh->cgnhp', t, eye).reshape(SSM_CHUNKS, gpc * SSM_STATE, gpc * SSM_GROUP)

    bw = jnp.concatenate([in_chunks(bbar_re), in_chunks(bbar_im)], axis=0).astype(BF16)
    cw = jnp.concatenate([out_chunks(c_re), out_chunks(-c_im)], axis=0).astype(BF16)
    are = jnp.broadcast_to(abar_re.reshape(1, SSM_LANES), (bsz, SSM_LANES))
    aim = jnp.broadcast_to(abar_im.reshape(1, SSM_LANES), (bsz, SSM_LANES))
    return bw, cw, are, aim


def _merge_kernel(x_ref, a_ref, s_ref, p_ref, win_ref, bg_ref, wa_ref, ws_ref, wp_ref, wo_ref, g_ref, b_ref, o_ref):
    x = x_ref[...]
    xb = x.astype(BF16)
    merged = None
    for i, (br_ref, w_ref) in enumerate(((a_ref, wa_ref), (s_ref, ws_ref), (p_ref, wp_ref))):
        cols = slice(i * D_MODEL, (i + 1) * D_MODEL)
        wcols = slice(BRANCH_IN_WIDTH + i * D_MODEL, BRANCH_IN_WIDTH + (i + 1) * D_MODEL)
        gate = jax.nn.sigmoid(_dot(xb, win_ref[:, wcols]) + bg_ref[:, cols])
        term = gate * _dot(br_ref[...], w_ref[...])
        merged = term if merged is None else merged + term
    h = _dot(merged.astype(BF16), wo_ref[...])
    o_ref[...] = _layer_norm(DN_ALPHA * x + h, g_ref[...], b_ref[...])


def _merge(x, a, s, p, stacks, *, layer, tm):
    n, d = x.shape
    row = lambda w: pl.BlockSpec((tm, w), lambda i: (i, 0))
    return pl.pallas_call(
        _merge_kernel,
        grid=(n // tm,),
        in_specs=[row(d), row(Q_WIDTH), row(SSM_WIDTH), row(POOL_WIDTH)] + [_layer_spec(c, layer) for c in stacks],
        out_specs=row(d),
        out_shape=jax.ShapeDtypeStruct((n, d), F32),
        compiler_params=_params(("parallel",)),
        name="merge",
    )(x, a, s, p, *stacks)


def _matmul_kernel(a_ref, w_ref, o_ref):
    o_ref[...] = _dot(a_ref[...].astype(BF16), w_ref[...]).astype(o_ref.dtype)


def _matmul(a, w, *, layer, tm, out_dtype):
    m, k = a.shape
    n = w.shape[-1]
    return pl.pallas_call(
        _matmul_kernel,
        grid=(m // tm,),
        in_specs=[pl.BlockSpec((tm, k), lambda i: (i, 0)), _layer_spec(w, layer)],
        out_specs=pl.BlockSpec((tm, n), lambda i: (i, 0)),
        out_shape=jax.ShapeDtypeStruct((m, n), out_dtype),
        compiler_params=_params(("parallel",)),
        name="kvproj",
    )(a, w)


def _xattn_kernel(x_ref, k_ref, v_ref, wq_ref, wo_ref, g_ref, b_ref, o_ref):
    x = x_ref[...]
    q = (_dot(x.astype(BF16), wq_ref[...]) * (X_HEAD_DIM ** -0.5)).astype(BF16)
    outs = []
    for h in range(X_HEADS):
        cols = slice(h * X_HEAD_DIM, (h + 1) * X_HEAD_DIM)
        s = _dot_nt(q[:, cols], k_ref[:, cols])
        p = jnp.exp(s - jnp.max(s, axis=-1, keepdims=True))
        inv = 1.0 / jnp.sum(p, axis=-1, keepdims=True)
        outs.append((_dot(p.astype(BF16), v_ref[:, cols]) * inv).astype(BF16))
    c = _dot(jnp.concatenate(outs, axis=1), wo_ref[...])
    o_ref[...] = _layer_norm(DN_ALPHA * x + c, g_ref[...], b_ref[...])


def _xattn(x, kv, stacks, *, layer, tm):
    bsz, seq, d = x.shape
    nmem = kv.shape[1]
    row = pl.BlockSpec((None, tm, d), lambda bi, t: (bi, t, 0))
    kspec = pl.BlockSpec((None, nmem, d), lambda bi, t: (bi, 0, 0))
    vspec = pl.BlockSpec((None, nmem, d), lambda bi, t: (bi, 0, 1))
    return pl.pallas_call(
        _xattn_kernel,
        grid=(bsz, seq // tm),
        in_specs=[row, kspec, vspec] + [_layer_spec(c, layer) for c in stacks],
        out_specs=row,
        out_shape=jax.ShapeDtypeStruct((bsz, seq, d), F32),
        compiler_params=_params(("parallel", "parallel")),
        name="xattn",
    )(x, kv, kv, *stacks)


def _ff_chunks(width, nchunk):
    tiles = width // MXU_TILE
    assert tiles * MXU_TILE == width
    bounds = [MXU_TILE * ((tiles * c) // nchunk) for c in range(nchunk + 1)]
    return [slice(lo, hi) for lo, hi in zip(bounds[:-1], bounds[1:])]


def _swiglu(xb, wg_ref, wu_ref, wd_ref, nchunk):
    acc = None
    for cols in _ff_chunks(wg_ref.shape[1], nchunk):
        hid = jax.nn.silu(_dot(xb, wg_ref[:, cols])) * _dot(xb, wu_ref[:, cols])
        part = _dot(hid.astype(BF16), wd_ref[cols, :])
        acc = part if acc is None else acc + part
    return acc


def _ffn_kernel(x_ref, wg_ref, wu_ref, wd_ref, g_ref, b_ref, o_ref, *, nchunk):
    x = x_ref[...]
    acc = _swiglu(x.astype(BF16), wg_ref, wu_ref, wd_ref, nchunk)
    o_ref[...] = _layer_norm(DN_ALPHA * x + acc, g_ref[...], b_ref[...])


def _ffn(x, wgu, wd, g, b, *, layer, ffn_layer, tm, nchunk):
    n, d = x.shape
    ff = wd.shape[1]
    row = pl.BlockSpec((tm, d), lambda i: (i, 0))
    return pl.pallas_call(
        functools.partial(_ffn_kernel, nchunk=nchunk),
        grid=(n // tm,),
        in_specs=[row, _layer_spec(wgu, ffn_layer, width=ff, col=0), _layer_spec(wgu, ffn_layer, width=ff, col=1),
                  _layer_spec(wd, ffn_layer), _layer_spec(g, layer), _layer_spec(b, layer)],
        out_specs=row,
        out_shape=jax.ShapeDtypeStruct((n, d), F32),
        compiler_params=_params(("parallel",)),
        name="ffn",
    )(x, wgu, wgu, wd, g, b)


MOE_TT = 512
SEG_ALIGN = 16
MOE_RB = 512
MOE_RTILE = 2 * MOE_TT + N_EXPERTS * SEG_ALIGN
MOE_FF_CHUNKS = 2
R_E1, R_E2, R_W1, R_W2, R_K1, R_K2 = range(6)


def _router_kernel(x_ref, wr_ref, br_ref, route_ref, routet_ref, cnt_ref, *, tm):
    logits = lax.dot_general(wr_ref[...], x_ref[...], (((1,), (1,)), ((), ())),
                             precision=lax.Precision.HIGHEST, preferred_element_type=F32) + br_ref[...]
    eid = lax.broadcasted_iota(jnp.int32, logits.shape, 0)
    v1 = jnp.max(logits, axis=0, keepdims=True)
    e1 = jnp.min(jnp.where(logits == v1, eid, N_EXPERTS), axis=0, keepdims=True)
    rest = jnp.where(eid == e1, -jnp.inf, logits)
    v2 = jnp.max(rest, axis=0, keepdims=True)
    e2 = jnp.min(jnp.where(rest == v2, eid, N_EXPERTS), axis=0, keepdims=True)
    t = jnp.exp(v2 - v1)
    w1 = 1.0 / (1.0 + t)
    w2 = t / (1.0 + t)
    sel1, sel2 = eid == e1, eid == e2
    member = (sel1 | sel2).astype(BF16)
    before = (lax.broadcasted_iota(jnp.int32, (tm, tm), 0) < lax.broadcasted_iota(jnp.int32, (tm, tm), 1))
    rank = _dot(member, before.astype(BF16))
    k1 = jnp.sum(jnp.where(sel1, rank, 0.0), axis=0, keepdims=True)
    k2 = jnp.sum(jnp.where(sel2, rank, 0.0), axis=0, keepdims=True)
    route = jnp.zeros(logits.shape, F32)
    for r, val in ((R_E1, e1.astype(F32)), (R_E2, e2.astype(F32)), (R_W1, w1), (R_W2, w2), (R_K1, k1), (R_K2, k2)):
        route = jnp.where(eid == r, val, route)
    route_ref[...] = route
    padded = jnp.concatenate([route, jnp.zeros((LANE - N_EXPERTS, tm), F32)], axis=0)
    routet_ref[...] = padded.T
    cnt = jnp.sum(member.astype(F32), axis=1, keepdims=True)
    cnt_ref[...] = jnp.broadcast_to(cnt, (N_EXPERTS, LANE))


def _router(x, wr_t, br, *, layer, tm):
    n, d = x.shape
    nt = n // tm
    return pl.pallas_call(
        functools.partial(_router_kernel, tm=tm),
        grid=(nt,),
        in_specs=[pl.BlockSpec((tm, d), lambda i: (i, 0)), _layer_spec(wr_t, layer), _layer_spec(br, layer)],
        out_specs=[pl.BlockSpec((N_EXPERTS, tm), lambda i: (0, i)),
                   pl.BlockSpec((tm, LANE), lambda i: (i, 0)),
                   pl.BlockSpec((None, N_EXPERTS, LANE), lambda i: (i, 0, 0))],
        out_shape=[jax.ShapeDtypeStruct((N_EXPERTS, n), F32),
                   jax.ShapeDtypeStruct((n, LANE), F32),
                   jax.ShapeDtypeStruct((nt, N_EXPERTS, LANE), F32)],
        compiler_params=_params(("parallel",)),
        name="router",
    )(x, wr_t, br)


def _start_segments(pc_ref, off_ref, tile, make_copy):
    loc = jnp.int32(0)
    for e in range(N_EXPERTS):
        npieces = pc_ref[tile * N_EXPERTS + e] // SEG_ALIGN
        glob = off_ref[tile * N_EXPERTS + e]

        def start(p, carry, loc=loc, glob=glob):
            make_copy(pl.multiple_of(loc + p * SEG_ALIGN, SEG_ALIGN),
                      pl.multiple_of(glob + p * SEG_ALIGN, SEG_ALIGN)).start()
            return carry

        lax.fori_loop(0, npieces, start, 0)
        loc = loc + npieces * SEG_ALIGN


def _wait_segments(pc_ref, tile, make_copy):
    total = jnp.int32(0)
    for e in range(N_EXPERTS):
        total = total + pc_ref[tile * N_EXPERTS + e] // SEG_ALIGN

    def wait(p, carry):
        make_copy(0, 0).wait()
        return carry

    lax.fori_loop(0, total, wait, 0)


def _tile_offsets(pc_ref, tile):
    locs, acc = [], jnp.int32(0)
    for e in range(N_EXPERTS):
        locs.append(acc)
        acc = acc + pc_ref[tile * N_EXPERTS + e]
    return locs


def _dest_rows(expert, rank, locs):
    base = jnp.zeros(expert.shape, F32)
    for e in range(N_EXPERTS):
        base = jnp.where(expert == float(e), locs[e].astype(F32), base)
    return (base + rank).astype(jnp.int32)


def _dispatch_kernel(pc_ref, off_ref, tail_ref, nused_ref, x_ref, route_ref, buf_ref, stage_ref, zero_ref, sem_ref,
                     *, nblocks):
    i = pl.program_id(0)
    last = pl.num_programs(0) - 1
    slot = i % 2
    locs = _tile_offsets(pc_ref, i)
    d1 = _dest_rows(route_ref[R_E1:R_E1 + 1, :], route_ref[R_K1:R_K1 + 1, :], locs)
    d2 = _dest_rows(route_ref[R_E2:R_E2 + 1, :], route_ref[R_K2:R_K2 + 1, :], locs)
    rows = lax.broadcasted_iota(jnp.int32, (MOE_RTILE, MOE_TT), 0)
    onehot = ((rows == d1) | (rows == d2)).astype(BF16)
    stage_ref[slot] = _dot(onehot, x_ref[...].astype(BF16)).astype(BF16)

    def seg_copy(s):
        return lambda loc, glob: pltpu.make_async_copy(
            stage_ref.at[s, pl.ds(loc, SEG_ALIGN), :], buf_ref.at[pl.ds(glob, SEG_ALIGN), :], sem_ref.at[s])

    _start_segments(pc_ref, off_ref, i, seg_copy(slot))

    @pl.when(i > 0)
    def _():
        _wait_segments(pc_ref, i - 1, seg_copy(1 - slot))

    @pl.when(i == last)
    def _():
        _wait_segments(pc_ref, i, seg_copy(slot))
        zero_ref[...] = jnp.zeros(zero_ref.shape, BF16)

        def piece(glob):
            return pltpu.make_async_copy(zero_ref.at[pl.ds(0, SEG_ALIGN), :],
                                         buf_ref.at[pl.ds(glob, SEG_ALIGN), :], sem_ref.at[2])

        def block(row):
            return pltpu.make_async_copy(zero_ref, buf_ref.at[pl.ds(row, MOE_RB), :], sem_ref.at[2])

        npieces = jnp.int32(0)
        for e in range(N_EXPERTS):
            start, cnt = tail_ref[e], tail_ref[N_EXPERTS + e]
            lax.fori_loop(0, cnt, lambda p, c, start=start: (
                piece(pl.multiple_of(start + p * SEG_ALIGN, SEG_ALIGN)).start(), c)[1], 0)
            npieces = npieces + cnt
        lax.fori_loop(0, npieces, lambda p, c: (piece(0).wait(), c)[1], 0)
        lax.fori_loop(nused_ref[0], nblocks,
                      lambda b, c: (block(pl.multiple_of(b * MOE_RB, MOE_RB)).start(), c)[1], 0)
        lax.fori_loop(nused_ref[0], nblocks, lambda b, c: (block(0).wait(), c)[1], 0)


def _dispatch(x, route, pc, off, tail, nused, rmax):
    n, d = x.shape
    return pl.pallas_call(
        functools.partial(_dispatch_kernel, nblocks=rmax // MOE_RB),
        grid_spec=pltpu.PrefetchScalarGridSpec(
            num_scalar_prefetch=4,
            grid=(n // MOE_TT,),
            in_specs=[pl.BlockSpec((MOE_TT, d), lambda i, *_: (i, 0)),
                      pl.BlockSpec((N_EXPERTS, MOE_TT), lambda i, *_: (0, i))],
            out_specs=pl.BlockSpec(memory_space=pl.ANY),
            scratch_shapes=[pltpu.VMEM((2, MOE_RTILE, d), BF16), pltpu.VMEM((MOE_RB, d), BF16),
                            pltpu.SemaphoreType.DMA((3,))]),
        out_shape=jax.ShapeDtypeStruct((rmax, d), BF16),
        compiler_params=_params(("arbitrary",)),
        name="dispatch",
    )(pc, off, tail, nused, x, route)


def _experts_kernel(be_ref, nused_ref, x_ref, wg_ref, wu_ref, wd_ref, o_ref):
    @pl.when(pl.program_id(0) < nused_ref[0])
    def _():
        o_ref[...] = _swiglu(x_ref[...], wg_ref, wu_ref, wd_ref, MOE_FF_CHUNKS).astype(BF16)

    @pl.when(pl.program_id(0) >= nused_ref[0])
    def _():
        o_ref[...] = jnp.zeros(o_ref.shape, BF16)


def _experts(buf, be, nused, wgu, wd, *, moe_layer):
    rmax, d = buf.shape
    ff = wd.shape[2]
    blk = pl.BlockSpec((MOE_RB, d), lambda i, be, nu: (jnp.maximum(jnp.minimum(i, nu[0] - 1), 0), 0))
    out_blk = pl.BlockSpec((MOE_RB, d), lambda i, be, nu: (i, 0))
    once = pl.Buffered(1)
    return pl.pallas_call(
        _experts_kernel,
        grid_spec=pltpu.PrefetchScalarGridSpec(
            num_scalar_prefetch=2,
            grid=(rmax // MOE_RB,),
            in_specs=[blk,
                      pl.BlockSpec((None, None, d, ff), lambda i, be, nu: (moe_layer, be[i], 0, 0), pipeline_mode=once),
                      pl.BlockSpec((None, None, d, ff), lambda i, be, nu: (moe_layer, be[i], 0, 1), pipeline_mode=once),
                      pl.BlockSpec((None, None, ff, d), lambda i, be, nu: (moe_layer, be[i], 0, 0), pipeline_mode=once)],
            out_specs=out_blk),
        out_shape=jax.ShapeDtypeStruct((rmax, d), BF16),
        compiler_params=_params(("arbitrary",)),
        name="experts",
    )(be, nused, buf, wgu, wgu, wd)


def _combine_kernel(pc_ref, off_ref, x_ref, routet_ref, g_ref, b_ref, buf_ref, o_ref, stage_ref, sem_ref):
    i = pl.program_id(0)
    last = pl.num_programs(0) - 1
    slot = i % 2

    def seg_copy(s):
        return lambda loc, glob: pltpu.make_async_copy(
            buf_ref.at[pl.ds(glob, SEG_ALIGN), :], stage_ref.at[s, pl.ds(loc, SEG_ALIGN), :], sem_ref.at[s])

    @pl.when(i == 0)
    def _():
        stage_ref[...] = jnp.zeros(stage_ref.shape, BF16)
        _start_segments(pc_ref, off_ref, i, seg_copy(slot))

    _wait_segments(pc_ref, i, seg_copy(slot))

    @pl.when(i < last)
    def _():
        _start_segments(pc_ref, off_ref, i + 1, seg_copy(1 - slot))

    locs = _tile_offsets(pc_ref, i)
    rt = routet_ref[...]
    col = lambda r: rt[:, r:r + 1]
    d1 = _dest_rows(col(R_E1), col(R_K1), locs)
    d2 = _dest_rows(col(R_E2), col(R_K2), locs)
    lanes = lax.broadcasted_iota(jnp.int32, (MOE_TT, MOE_RTILE), 1)
    y = stage_ref[slot]
    f = col(R_W1) * _dot((lanes == d1).astype(BF16), y) + col(R_W2) * _dot((lanes == d2).astype(BF16), y)
    o_ref[...] = _layer_norm(DN_ALPHA * x_ref[...] + f, g_ref[...], b_ref[...])


def _combine(x, routet, pc, off, buf, g, b, *, layer):
    n, d = x.shape
    row = pl.BlockSpec((MOE_TT, d), lambda i, *_: (i, 0))
    return pl.pallas_call(
        _combine_kernel,
        grid_spec=pltpu.PrefetchScalarGridSpec(
            num_scalar_prefetch=2,
            grid=(n // MOE_TT,),
            in_specs=[row, pl.BlockSpec((MOE_TT, LANE), lambda i, *_: (i, 0)),
                      _layer_spec(g, layer), _layer_spec(b, layer), pl.BlockSpec(memory_space=pl.ANY)],
            out_specs=row,
            scratch_shapes=[pltpu.VMEM((2, MOE_RTILE, d), BF16), pltpu.SemaphoreType.DMA((2,))]),
        out_shape=jax.ShapeDtypeStruct((n, d), F32),
        compiler_params=_params(("arbitrary",)),
        name="combine",
    )(pc, off, x, routet, g, b, buf)


def _moe(x, wr_t, b_router, wgu, wd, g, b, *, layer, moe_layer):
    n, d = x.shape
    nt = n // MOE_TT
    route, routet, cnt = _router(x, wr_t, b_router, layer=moe_layer, tm=MOE_TT)

    cnt = cnt[:, :, 0].astype(jnp.int32)
    pc = (cnt + SEG_ALIGN - 1) // SEG_ALIGN * SEG_ALIGN
    total = jnp.sum(pc, axis=0)
    region = (total + MOE_RB - 1) // MOE_RB * MOE_RB
    base = jnp.cumsum(region) - region
    off = base[None, :] + jnp.cumsum(pc, axis=0) - pc
    rmax = (2 * n + nt * N_EXPERTS * (SEG_ALIGN - 1) + N_EXPERTS * (MOE_RB - 1) + MOE_RB - 1) // MOE_RB * MOE_RB
    blk_end = jnp.cumsum(region // MOE_RB)
    nused = blk_end[-1:].astype(jnp.int32)
    blk = jnp.minimum(jnp.arange(rmax // MOE_RB, dtype=jnp.int32), nused - 1)
    be = jnp.sum(blk[:, None] >= blk_end[None, :], axis=1).astype(jnp.int32)
    tail = jnp.concatenate([base + total, (region - total) // SEG_ALIGN]).astype(jnp.int32)
    pc, off = pc.reshape(-1), off.reshape(-1).astype(jnp.int32)

    buf = _dispatch(x, route, pc, off, tail, nused, rmax)
    y = _experts(buf, be, nused, wgu, wd, moe_layer=moe_layer)
    return _combine(x, routet, pc, off, y, g, b, layer=layer)


def _rotary_tables(seq):
    half = ROT_DIM // 2
    inv_freq = jnp.power(jnp.float32(ROPE_THETA), -jnp.arange(half, dtype=F32) / half)
    ang = jnp.arange(seq, dtype=jnp.int32).astype(F32)[:, None] * inv_freq[None, :]
    cos, sin = jnp.cos(ang), jnp.sin(ang)
    d = jnp.arange(LANE) % HEAD_DIM
    first, second = d < half, (d >= half) & (d < ROT_DIM)
    idx = d % half
    cos_t = jnp.where((first | second)[None, :], cos[:, idx], 1.0)
    sin_a = jnp.where(first[None, :], -sin[:, idx], 0.0)
    sin_b = jnp.where(second[None, :], sin[:, idx], 0.0)
    return cos_t, sin_a, sin_b


def kernel(x, mem, w_in, b_gate, attn_sinks, ssm_a_re, ssm_a_im, ssm_log_dt, ssm_b_re, ssm_b_im, ssm_c_re, ssm_c_im, ssm_d, ssm_w_glu, pool_w, pool_scale, w_br_attn, w_br_ssm, w_br_pool, w_o, ln1_g, ln1_b, xa_wq, xa_wkv, xa_wo, ln2_g, ln2_b, ffn_w_gu, ffn_w_down, moe_w_router, moe_b_router, moe_w_gu, moe_w_down, ln3_g, ln3_b):
    bsz, seq, d = x.shape
    n = bsz * seq
    nmem = mem.shape[1]
    cos_t, sin_a, sin_b = _rotary_tables(seq)
    mem2 = mem.reshape(bsz * nmem, d)
    bf = lambda w: w.astype(BF16)
    vec = lambda v: v.reshape(v.shape[0], 1, v.shape[1])

    w_in_b, pool_w_b, wglu_b = bf(w_in), bf(pool_w), bf(ssm_w_glu)
    merge_stacks = (w_in_b, vec(b_gate), bf(w_br_attn), bf(w_br_ssm), bf(w_br_pool), bf(w_o), vec(ln1_g), vec(ln1_b))
    xattn_stacks = (bf(xa_wq), bf(xa_wo), vec(ln2_g), vec(ln2_b))
    wkv_b = bf(xa_wkv)
    ffn_gu_b, ffn_down_b = bf(ffn_w_gu), bf(ffn_w_down)
    moe_gu_b, moe_down_b = bf(moe_w_gu), bf(moe_w_down)
    router_t = jnp.swapaxes(moe_w_router, 1, 2)
    router_b = moe_b_router.reshape(moe_b_router.shape[0], N_EXPERTS, 1)
    pool_scale3, ssm_d3, ln3_g3, ln3_b3 = vec(pool_scale), vec(ssm_d), vec(ln3_g), vec(ln3_b)

    for i in range(DEPTH):
        q, k, v, u, pm = _inproj(x, w_in_b, cos_t, sin_a, sin_b, pool_w_b, pool_scale3,
                                 layer=i, tt=min(TT_INPROJ, seq))
        attn = _swa(q, k, v, attn_sinks, layer=i, tq=min(TQ_SWA, seq))
        bw, cw, are, aim = _ssm_weights(ssm_a_re[i], ssm_a_im[i], ssm_log_dt[i], ssm_b_re[i], ssm_b_im[i],
                                        ssm_c_re[i], ssm_c_im[i], bsz)
        hs = _ssm(u, bw, cw, are, aim, ssm_d3, wglu_b, layer=i, tc=TC_SSM)
        x2 = _merge(x.reshape(n, d), attn.reshape(n, -1), hs.reshape(n, -1), pm.reshape(n, -1), merge_stacks,
                    layer=i, tm=TM_MERGE)

        kv = _matmul(mem2, wkv_b, layer=i, tm=512, out_dtype=BF16).reshape(bsz, nmem, 2 * d)
        x3 = _xattn(x2.reshape(bsz, seq, d), kv, xattn_stacks, layer=i, tm=min(TM_XATTN, seq)).reshape(n, d)

        j = i // 2
        if i % 2 == 0:
            x4 = _ffn(x3, ffn_gu_b, ffn_down_b, ln3_g3, ln3_b3, layer=i, ffn_layer=j, tm=TM_FFN, nchunk=FFN_CHUNKS)
        else:
            x4 = _moe(x3, router_t, router_b, moe_gu_b, moe_down_b, ln3_g3, ln3_b3, layer=i, moe_layer=j)
        x = x4.reshape(bsz, seq, d)
    return x
```

```python
import functools

import jax
import jax.numpy as jnp
from jax import lax
from jax.experimental import pallas as pl
from jax.experimental.pallas import tpu as pltpu

D_MODEL = 1024
DEPTH = 4
N_Q_HEADS = 8
N_KV_HEADS = 2
HEAD_DIM = 64
WINDOW = 128
ROT_DIM = HEAD_DIM // 4
ROPE_THETA = 500000.0
SSM_WIDTH = D_MODEL // 2
SSM_GROUP = 16
SSM_GROUPS = SSM_WIDTH // SSM_GROUP
SSM_STATE = 64
SSM_LANES = SSM_GROUPS * SSM_STATE
POOL_WINDOWS = (2, 4, 8, 16)
POOL_WIDTH = D_MODEL // 2
POOL_GROUP = POOL_WIDTH // len(POOL_WINDOWS)
POOL_HALO = 16
X_HEADS = 4
X_HEAD_DIM = D_MODEL // X_HEADS
D_FF = 2816
N_EXPERTS = 8
D_FF_EXPERT = 3584
Q_WIDTH = N_Q_HEADS * HEAD_DIM
KV_WIDTH = N_KV_HEADS * HEAD_DIM
QKV_WIDTH = Q_WIDTH + 2 * KV_WIDTH
BRANCH_IN_WIDTH = QKV_WIDTH + SSM_WIDTH + POOL_WIDTH
DN_ALPHA = (2.0 * DEPTH) ** 0.25
LN_EPS = 1e-5
NEG_INF = -1e30

TT_INPROJ = 1024
TQ_SWA = 1024
TC_SSM = 128
TM_MERGE = 1024
TM_XATTN = 1024
TM_FFN = 1024
FFN_CHUNKS = 2

LANE = 128
MXU_TILE = 256
VMEM_LIMIT = 56 * 1024 * 1024

BF16 = jnp.bfloat16
F32 = jnp.float32


def _dot(a, b):
    return jnp.dot(a, b, preferred_element_type=F32)


def _dot_nt(a, b):
    return lax.dot_general(a, b, (((1,), (1,)), ((), ())), preferred_element_type=F32)


def _layer_norm(y, g, b):
    mu = jnp.mean(y, axis=-1, keepdims=True)
    d = y - mu
    var = jnp.mean(d * d, axis=-1, keepdims=True)
    return d * lax.rsqrt(var + LN_EPS) * g + b


def _const_spec(shape):
    nd = len(shape)
    return pl.BlockSpec(shape, lambda *_: (0,) * nd, pipeline_mode=pl.Buffered(1))


def _layer_spec(stack, layer, width=None, col=0):
    rest = stack.shape[1:]
    block = (None,) + rest[:-1] + (rest[-1] if width is None else width,)
    index = (layer,) + (0,) * (len(rest) - 1) + (col,)
    return pl.BlockSpec(block, lambda *_: index, pipeline_mode=pl.Buffered(1))


def _params(sem, limit=VMEM_LIMIT):
    return pltpu.CompilerParams(dimension_semantics=sem, vmem_limit_bytes=limit)


def _rotary(t, cos, sin_a, sin_b):
    return t * cos + pltpu.roll(t, LANE - ROT_DIM // 2, 1) * sin_a + pltpu.roll(t, ROT_DIM // 2, 1) * sin_b


def _inproj_kernel(x_ref, w_ref, cos_ref, sa_ref, sb_ref, pw_ref, ps_ref,
                   q_ref, k_ref, v_ref, u_ref, pm_ref, hist_ref, *, tt):
    ti = pl.program_id(1)
    xb = x_ref[...].astype(BF16)
    cos, sa, sb = cos_ref[...], sa_ref[...], sb_ref[...]

    qkv = _dot(xb, w_ref[:, 0:QKV_WIDTH])
    for c in range(Q_WIDTH // LANE):
        t = qkv[:, c * LANE:(c + 1) * LANE] * (HEAD_DIM ** -0.5)
        q_ref[:, c * LANE:(c + 1) * LANE] = _rotary(t, cos, sa, sb).astype(BF16)
    k_ref[...] = _rotary(qkv[:, Q_WIDTH:Q_WIDTH + KV_WIDTH], cos, sa, sb).astype(BF16)
    v_ref[...] = qkv[:, Q_WIDTH + KV_WIDTH:].astype(BF16)

    u_ref[...] = _dot(xb, w_ref[:, QKV_WIDTH:QKV_WIDTH + SSM_WIDTH])

    @pl.when(ti == 0)
    def _():
        hist_ref[0:POOL_HALO, :] = jnp.zeros((POOL_HALO, POOL_WIDTH), F32)

    up = _dot(xb, w_ref[:, QKV_WIDTH + SSM_WIDTH:BRANCH_IN_WIDTH])
    hist_ref[POOL_HALO:POOL_HALO + tt, :] = up
    pos1 = (ti * tt + 1 + lax.broadcasted_iota(jnp.int32, (tt, 1), 0)).astype(F32)
    for gi, w in enumerate(POOL_WINDOWS):
        lanes = slice(gi * POOL_GROUP, (gi + 1) * POOL_GROUP)
        acc = hist_ref[:, lanes]
        k = 1
        while k < w:
            acc = acc + pltpu.roll(acc, k, 0)
            k *= 2
        pooled = acc[POOL_HALO:, :] / jnp.minimum(pos1, float(w)) - up[:, lanes]
        mixed = _dot(pooled.astype(BF16), pw_ref[gi]) * ps_ref[:, lanes]
        pm_ref[:, lanes] = mixed.astype(BF16)
    hist_ref[0:POOL_HALO, :] = hist_ref[tt:tt + POOL_HALO, :]


def _inproj(x, w_in, cos, sa, sb, pool_w, pool_scale, *, layer, tt):
    bsz, seq, d = x.shape
    row = lambda w: pl.BlockSpec((None, tt, w), lambda b, t: (b, t, 0))
    tab = pl.BlockSpec((tt, LANE), lambda b, t: (t, 0))
    return pl.pallas_call(
        functools.partial(_inproj_kernel, tt=tt),
        grid=(bsz, seq // tt),
        in_specs=[row(d), _layer_spec(w_in, layer, width=BRANCH_IN_WIDTH), tab, tab, tab,
                  _layer_spec(pool_w, layer), _layer_spec(pool_scale, layer)],
        out_specs=[row(Q_WIDTH), row(KV_WIDTH), row(KV_WIDTH), row(SSM_WIDTH), row(POOL_WIDTH)],
        out_shape=[jax.ShapeDtypeStruct((bsz, seq, Q_WIDTH), BF16),
                   jax.ShapeDtypeStruct((bsz, seq, KV_WIDTH), BF16),
                   jax.ShapeDtypeStruct((bsz, seq, KV_WIDTH), BF16),
                   jax.ShapeDtypeStruct((bsz, seq, SSM_WIDTH), F32),
                   jax.ShapeDtypeStruct((bsz, seq, POOL_WIDTH), BF16)],
        scratch_shapes=[pltpu.VMEM((tt + POOL_HALO, POOL_WIDTH), F32)],
        compiler_params=_params(("arbitrary", "arbitrary")),
        name="inproj",
    )(x, w_in, cos, sa, sb, pool_w, pool_scale)


def _half_lanes(a, b, lo):
    return jnp.where(lo, a, b)


def _swa_kernel(sink_ref, q_ref, k_ref, kp_ref, v_ref, vp_ref, o_ref, *, nsub):
    ti = pl.program_id(1)
    blk = WINDOW
    lo = lax.broadcasted_iota(jnp.int32, (2 * blk, LANE), 1) < HEAD_DIM
    tri = (lax.broadcasted_iota(jnp.int32, (2 * blk, blk), 1)
           <= lax.broadcasted_iota(jnp.int32, (2 * blk, blk), 0) % blk)
    top = lax.broadcasted_iota(jnp.int32, (2 * blk, 1), 0) < blk
    zero = jnp.zeros((2 * blk, LANE), F32)
    ones_sel = ((lax.broadcasted_iota(jnp.int32, (4 * blk, LANE), 0) < 2 * blk)
                == (lax.broadcasted_iota(jnp.int32, (4 * blk, LANE), 1) < HEAD_DIM)).astype(BF16)

    for j in range(nsub):
        if j == 0:
            kc = jnp.concatenate([kp_ref[...], k_ref[0:blk, :]], axis=0)
            vc = jnp.concatenate([vp_ref[...], v_ref[0:blk, :]], axis=0)
        else:
            kc = k_ref[(j - 1) * blk:(j + 1) * blk, :]
            vc = v_ref[(j - 1) * blk:(j + 1) * blk, :]
        kc = kc.astype(F32)
        vc = vc.astype(F32)
        kr = pltpu.roll(kc, HEAD_DIM, 1)
        vr = pltpu.roll(vc, HEAD_DIM, 1)
        kmat = [[_half_lanes(kc, zero, lo), _half_lanes(zero, kr, lo)],
                [_half_lanes(kr, zero, lo), _half_lanes(zero, kc, lo)]]
        vmat = [[_half_lanes(vc, zero, lo), _half_lanes(zero, vr, lo)],
                [_half_lanes(vr, zero, lo), _half_lanes(zero, vc, lo)]]
        rows = pl.ds(j * blk, blk)
        for kh in range(N_KV_HEADS):
            base = kh * 2 * LANE
            qq = jnp.concatenate([q_ref[rows, base:base + LANE], q_ref[rows, base + LANE:base + 2 * LANE]], axis=0)
            probs, sink_term = [], []
            for par in range(2):
                s = _dot_nt(qq, kmat[kh][par].astype(BF16))
                s_prev = s[:, :blk] if j > 0 else jnp.where(ti > 0, s[:, :blk], NEG_INF)
                s = jnp.where(tri, s[:, blk:], s_prev)
                h_top = kh * 4 + par
                sink = jnp.where(top, sink_ref[h_top], sink_ref[h_top + 2])
                m = jnp.maximum(jnp.max(s, axis=-1, keepdims=True), sink)
                p = jnp.exp(s - m)
                sink_term.append(jnp.exp(sink - m))
                probs.append(jnp.concatenate([jnp.where(tri, 0.0, p), jnp.where(tri, p, 0.0)], axis=1).astype(BF16))
            pcat = jnp.concatenate(probs, axis=1)
            vcat = jnp.concatenate([vmat[kh][0], vmat[kh][1]], axis=0).astype(BF16)
            ov = _dot(pcat, jnp.concatenate([vcat, ones_sel], axis=1))
            den = ov[:, LANE:] + jnp.where(lo, sink_term[0], sink_term[1])
            o = ov[:, :LANE] / den
            o_ref[rows, base:base + LANE] = o[0:blk].astype(BF16)
            o_ref[rows, base + LANE:base + 2 * LANE] = o[blk:2 * blk].astype(BF16)


def _swa(q, k, v, sinks, *, layer, tq):
    bsz, seq, _ = q.shape
    nsub = tq // WINDOW
    cur = lambda w: pl.BlockSpec((None, tq, w), lambda b, t: (b, t, 0))
    prev = pl.BlockSpec((None, WINDOW, KV_WIDTH), lambda b, t: (b, jnp.maximum(t * nsub - 1, 0), 0))
    return pl.pallas_call(
        functools.partial(_swa_kernel, nsub=nsub),
        grid=(bsz, seq // tq),
        in_specs=[pl.BlockSpec(memory_space=pltpu.SMEM), cur(Q_WIDTH), cur(KV_WIDTH), prev, cur(KV_WIDTH), prev],
        out_specs=cur(Q_WIDTH),
        out_shape=jax.ShapeDtypeStruct((bsz, seq, Q_WIDTH), BF16),
        compiler_params=_params(("parallel", "parallel")),
        name="swa",
    )(sinks[layer], q, k, k, v, v)


SSM_CHUNKS = 4
SSM_CHUNK_LANES = SSM_LANES // SSM_CHUNKS
SSM_CHUNK_IN = SSM_WIDTH // SSM_CHUNKS
SCAN_LANES = 1024


def _ssm_kernel(u_ref, bw_ref, cw_ref, are_ref, aim_ref, d_ref, wglu_ref, o_ref, xs_ref, st_ref, h_ref,
                *, tc, bsz):
    @pl.when(pl.program_id(0) == 0)
    def _():
        h_ref[...] = jnp.zeros(h_ref.shape, F32)

    for b in range(bsz):
        for c in range(SSM_CHUNKS):
            xs_ref[c, pl.ds(b, tc, stride=bsz), :] = u_ref[b, :, c * SSM_CHUNK_IN:(c + 1) * SSM_CHUNK_IN]

    for part in range(2):
        for c in range(SSM_CHUNKS):
            col = part * SSM_LANES + c * SSM_CHUNK_LANES
            st_ref[:, col:col + SSM_CHUNK_LANES] = _dot(xs_ref[c].astype(BF16), bw_ref[part * SSM_CHUNKS + c])

    for jb in range(SSM_LANES // SCAN_LANES):
        re = pl.ds(jb * SCAN_LANES, SCAN_LANES)
        im = pl.ds(SSM_LANES + jb * SCAN_LANES, SCAN_LANES)
        ar = are_ref[:, re]
        ai = aim_ref[:, re]

        def step(t, carry):
            hr, hi = carry
            r = pl.ds(pl.multiple_of(t * bsz, bsz), bsz)
            nr = ar * hr - ai * hi + st_ref[r, re]
            ni = ar * hi + ai * hr + st_ref[r, im]
            st_ref[r, re] = nr
            st_ref[r, im] = ni
            return nr, ni

        hr, hi = lax.fori_loop(0, tc, step, (h_ref[:, re], h_ref[:, im]), unroll=8)
        h_ref[:, re] = hr
        h_ref[:, im] = hi

    ys = []
    for c in range(SSM_CHUNKS):
        sre = st_ref[:, c * SSM_CHUNK_LANES:(c + 1) * SSM_CHUNK_LANES].astype(BF16)
        sim = st_ref[:, SSM_LANES + c * SSM_CHUNK_LANES:SSM_LANES + (c + 1) * SSM_CHUNK_LANES].astype(BF16)
        cols = slice(c * SSM_CHUNK_IN, (c + 1) * SSM_CHUNK_IN)
        y = _dot(sre, cw_ref[c]) + _dot(sim, cw_ref[SSM_CHUNKS + c])
        ys.append(y + d_ref[:, cols] * xs_ref[c])

    hh = jax.nn.gelu(jnp.concatenate(ys, axis=1)).astype(BF16)
    z = _dot(hh, wglu_ref[...])
    res = z[:, :SSM_WIDTH] * jax.nn.sigmoid(z[:, SSM_WIDTH:])
    for c in range(SSM_CHUNKS):
        xs_ref[c] = res[:, c * SSM_CHUNK_IN:(c + 1) * SSM_CHUNK_IN]
    for b in range(bsz):
        for c in range(SSM_CHUNKS):
            o_ref[b, :, c * SSM_CHUNK_IN:(c + 1) * SSM_CHUNK_IN] = (
                xs_ref[c, pl.ds(b, tc, stride=bsz), :].astype(BF16))


def _ssm(u, bw, cw, are, aim, d_skip, wglu, *, layer, tc):
    bsz, seq, _ = u.shape
    blk = pl.BlockSpec((bsz, tc, SSM_WIDTH), lambda t: (0, t, 0))
    return pl.pallas_call(
        functools.partial(_ssm_kernel, tc=tc, bsz=bsz),
        grid=(seq // tc,),
        in_specs=[blk, _const_spec(bw.shape), _const_spec(cw.shape), _const_spec(are.shape),
                  _const_spec(aim.shape), _layer_spec(d_skip, layer), _layer_spec(wglu, layer)],
        out_specs=blk,
        out_shape=jax.ShapeDtypeStruct((bsz, seq, SSM_WIDTH), BF16),
        scratch_shapes=[pltpu.VMEM((SSM_CHUNKS, tc * bsz, SSM_CHUNK_IN), F32),
                        pltpu.VMEM((tc * bsz, 2 * SSM_LANES), F32),
                        pltpu.VMEM((bsz, 2 * SSM_LANES), F32)],
        compiler_params=_params(("arbitrary",)),
        name="ssm",
    )(u, bw, cw, are, aim, d_skip, wglu)


def _ssm_weights(a_re, a_im, log_dt, b_re, b_im, c_re, c_im, bsz):
    dt = jnp.exp(log_dt)[:, None]
    decay = jnp.exp(dt * a_re)
    abar_re, abar_im = decay * jnp.cos(dt * a_im), decay * jnp.sin(dt * a_im)
    inv_abs2 = 1.0 / (a_re * a_re + a_im * a_im)
    num_re, num_im = abar_re - 1.0, abar_im
    f_re = (num_re * a_re + num_im * a_im) * inv_abs2
    f_im = (num_im * a_re - num_re * a_im) * inv_abs2
    bbar_re = f_re[..., None] * b_re - f_im[..., None] * b_im
    bbar_im = f_re[..., None] * b_im + f_im[..., None] * b_re
    gpc = SSM_GROUPS // SSM_CHUNKS
    eye = jnp.eye(gpc, dtype=F32)

    def in_chunks(bb):
        t = bb.reshape(SSM_CHUNKS, gpc, SSM_STATE, SSM_GROUP)
        return jnp.einsum('cgnp,gh->cgphn', t, eye).reshape(SSM_CHUNKS, gpc * SSM_GROUP, gpc * SSM_STATE)

    def out_chunks(cc):
        t = cc.reshape(SSM_CHUNKS, gpc, SSM_GROUP, SSM_STATE)
        return jnp.einsum('cgpn,gh->cgnhp', t, eye).reshape(SSM_CHUNKS, gpc * SSM_STATE, gpc * SSM_GROUP)

    bw = jnp.concatenate([in_chunks(bbar_re), in_chunks(bbar_im)], axis=0).astype(BF16)
    cw = jnp.concatenate([out_chunks(c_re), out_chunks(-c_im)], axis=0).astype(BF16)
    are = jnp.broadcast_to(abar_re.reshape(1, SSM_LANES), (bsz, SSM_LANES))
    aim = jnp.broadcast_to(abar_im.reshape(1, SSM_LANES), (bsz, SSM_LANES))
    return bw, cw, are, aim


def _merge_kernel(x_ref, a_ref, s_ref, p_ref, win_ref, bg_ref, wa_ref, ws_ref, wp_ref, wo_ref, g_ref, b_ref, o_ref):
    x = x_ref[...]
    xb = x.astype(BF16)
    merged = None
    for i, (br_ref, w_ref) in enumerate(((a_ref, wa_ref), (s_ref, ws_ref), (p_ref, wp_ref))):
        cols = slice(i * D_MODEL, (i + 1) * D_MODEL)
        wcols = slice(BRANCH_IN_WIDTH + i * D_MODEL, BRANCH_IN_WIDTH + (i + 1) * D_MODEL)
        gate = jax.nn.sigmoid(_dot(xb, win_ref[:, wcols]) + bg_ref[:, cols])
        term = gate * _dot(br_ref[...], w_ref[...])
        merged = term if merged is None else merged + term
    h = _dot(merged.astype(BF16), wo_ref[...])
    o_ref[...] = _layer_norm(DN_ALPHA * x + h, g_ref[...], b_ref[...])


def _merge(x, a, s, p, stacks, *, layer, tm):
    n, d = x.shape
    row = lambda w: pl.BlockSpec((tm, w), lambda i: (i, 0))
    return pl.pallas_call(
        _merge_kernel,
        grid=(n // tm,),
        in_specs=[row(d), row(Q_WIDTH), row(SSM_WIDTH), row(POOL_WIDTH)] + [_layer_spec(c, layer) for c in stacks],
        out_specs=row(d),
        out_shape=jax.ShapeDtypeStruct((n, d), F32),
        compiler_params=_params(("parallel",)),
        name="merge",
    )(x, a, s, p, *stacks)


def _matmul_kernel(a_ref, w_ref, o_ref):
    o_ref[...] = _dot(a_ref[...].astype(BF16), w_ref[...]).astype(o_ref.dtype)


def _matmul(a, w, *, layer, tm, out_dtype):
    m, k = a.shape
    n = w.shape[-1]
    return pl.pallas_call(
        _matmul_kernel,
        grid=(m // tm,),
        in_specs=[pl.BlockSpec((tm, k), lambda i: (i, 0)), _layer_spec(w, layer)],
        out_specs=pl.BlockSpec((tm, n), lambda i: (i, 0)),
        out_shape=jax.ShapeDtypeStruct((m, n), out_dtype),
        compiler_params=_params(("parallel",)),
        name="kvproj",
    )(a, w)


def _xattn_kernel(x_ref, k_ref, v_ref, wq_ref, wo_ref, g_ref, b_ref, o_ref):
    x = x_ref[...]
    q = (_dot(x.astype(BF16), wq_ref[...]) * (X_HEAD_DIM ** -0.5)).astype(BF16)
    outs = []
    for h in range(X_HEADS):
        cols = slice(h * X_HEAD_DIM, (h + 1) * X_HEAD_DIM)
        s = _dot_nt(q[:, cols], k_ref[:, cols])
        p = jnp.exp(s - jnp.max(s, axis=-1, keepdims=True))
        inv = 1.0 / jnp.sum(p, axis=-1, keepdims=True)
        outs.append((_dot(p.astype(BF16), v_ref[:, cols]) * inv).astype(BF16))
    c = _dot(jnp.concatenate(outs, axis=1), wo_ref[...])
    o_ref[...] = _layer_norm(DN_ALPHA * x + c, g_ref[...], b_ref[...])


def _xattn(x, kv, stacks, *, layer, tm):
    bsz, seq, d = x.shape
    nmem = kv.shape[1]
    row = pl.BlockSpec((None, tm, d), lambda bi, t: (bi, t, 0))
    kspec = pl.BlockSpec((None, nmem, d), lambda bi, t: (bi, 0, 0))
    vspec = pl.BlockSpec((None, nmem, d), lambda bi, t: (bi, 0, 1))
    return pl.pallas_call(
        _xattn_kernel,
        grid=(bsz, seq // tm),
        in_specs=[row, kspec, vspec] + [_layer_spec(c, layer) for c in stacks],
        out_specs=row,
        out_shape=jax.ShapeDtypeStruct((bsz, seq, d), F32),
        compiler_params=_params(("parallel", "parallel")),
        name="xattn",
    )(x, kv, kv, *stacks)


def _ff_chunks(width, nchunk):
    tiles = width // MXU_TILE
    assert tiles * MXU_TILE == width
    bounds = [MXU_TILE * ((tiles * c) // nchunk) for c in range(nchunk + 1)]
    return [slice(lo, hi) for lo, hi in zip(bounds[:-1], bounds[1:])]


def _swiglu(xb, wg_ref, wu_ref, wd_ref, nchunk):
    acc = None
    for cols in _ff_chunks(wg_ref.shape[1], nchunk):
        hid = jax.nn.silu(_dot(xb, wg_ref[:, cols])) * _dot(xb, wu_ref[:, cols])
        part = _dot(hid.astype(BF16), wd_ref[cols, :])
        acc = part if acc is None else acc + part
    return acc


def _ffn_kernel(x_ref, wg_ref, wu_ref, wd_ref, g_ref, b_ref, o_ref, *, nchunk):
    x = x_ref[...]
    acc = _swiglu(x.astype(BF16), wg_ref, wu_ref, wd_ref, nchunk)
    o_ref[...] = _layer_norm(DN_ALPHA * x + acc, g_ref[...], b_ref[...])


def _ffn(x, wgu, wd, g, b, *, layer, ffn_layer, tm, nchunk):
    n, d = x.shape
    ff = wd.shape[1]
    row = pl.BlockSpec((tm, d), lambda i: (i, 0))
    return pl.pallas_call(
        functools.partial(_ffn_kernel, nchunk=nchunk),
        grid=(n // tm,),
        in_specs=[row, _layer_spec(wgu, ffn_layer, width=ff, col=0), _layer_spec(wgu, ffn_layer, width=ff, col=1),
                  _layer_spec(wd, ffn_layer), _layer_spec(g, layer), _layer_spec(b, layer)],
        out_specs=row,
        out_shape=jax.ShapeDtypeStruct((n, d), F32),
        compiler_params=_params(("parallel",)),
        name="ffn",
    )(x, wgu, wgu, wd, g, b)


MOE_TT = 512
SEG_ALIGN = 16
SEG_BIG = 128
MOE_RB = 512
MOE_RTILE = 2 * MOE_TT + N_EXPERTS * SEG_ALIGN
MOE_FF_CHUNKS = 2
R_D1, R_D2, R_W1, R_W2 = range(4)


def _router_kernel(x_ref, wr_ref, br_ref, before_ref, route_ref, routet_ref, cnt_ref, *, tm):
    x, wr = x_ref[...], wr_ref[...]
    x_hi, wr_hi = x.astype(BF16), wr.astype(BF16)
    x_lo, wr_lo = (x - x_hi.astype(F32)).astype(BF16), (wr - wr_hi.astype(F32)).astype(BF16)
    logits = _dot_nt(wr_hi, x_hi) + _dot_nt(wr_hi, x_lo) + _dot_nt(wr_lo, x_hi) + br_ref[...]
    eid = lax.broadcasted_iota(jnp.int32, logits.shape, 0)
    v1 = jnp.max(logits, axis=0, keepdims=True)
    e1 = jnp.min(jnp.where(logits == v1, eid, N_EXPERTS), axis=0, keepdims=True)
    rest = jnp.where(eid == e1, -jnp.inf, logits)
    v2 = jnp.max(rest, axis=0, keepdims=True)
    e2 = jnp.min(jnp.where(rest == v2, eid, N_EXPERTS), axis=0, keepdims=True)
    t = jnp.exp(v2 - v1)
    w1 = 1.0 / (1.0 + t)
    w2 = t / (1.0 + t)
    sel1, sel2 = eid == e1, eid == e2
    member = (sel1 | sel2).astype(BF16)
    rank = _dot(member, before_ref[...])
    cnt = jnp.sum(member.astype(F32), axis=1, keepdims=True)
    cnt_ref[...] = jnp.broadcast_to(cnt, (N_EXPERTS, LANE))
    padded_cnt = jnp.floor((cnt + (SEG_ALIGN - 1)) * (1.0 / SEG_ALIGN)) * SEG_ALIGN
    seg = jnp.broadcast_to(padded_cnt, logits.shape)
    d1 = jnp.sum(jnp.where(eid < e1, seg, 0.0) + jnp.where(sel1, rank, 0.0), axis=0, keepdims=True)
    d2 = jnp.sum(jnp.where(eid < e2, seg, 0.0) + jnp.where(sel2, rank, 0.0), axis=0, keepdims=True)
    route = jnp.zeros(logits.shape, F32)
    for r, val in ((R_D1, d1), (R_D2, d2), (R_W1, w1), (R_W2, w2)):
        route = jnp.where(eid == r, val, route)
    route_ref[...] = route
    padded = jnp.concatenate([route, jnp.zeros((LANE - N_EXPERTS, tm), F32)], axis=0)
    routet_ref[...] = padded.T


def _router(x, wr_t, br, *, layer, tm):
    n, d = x.shape
    nt = n // tm
    before = (jnp.arange(tm)[:, None] < jnp.arange(tm)[None, :]).astype(BF16)
    return pl.pallas_call(
        functools.partial(_router_kernel, tm=tm),
        grid=(nt,),
        in_specs=[pl.BlockSpec((tm, d), lambda i: (i, 0)), _layer_spec(wr_t, layer), _layer_spec(br, layer),
                  _const_spec(before.shape)],
        out_specs=[pl.BlockSpec((N_EXPERTS, tm), lambda i: (0, i)),
                   pl.BlockSpec((tm, LANE), lambda i: (i, 0)),
                   pl.BlockSpec((None, N_EXPERTS, LANE), lambda i: (i, 0, 0))],
        out_shape=[jax.ShapeDtypeStruct((N_EXPERTS, n), F32),
                   jax.ShapeDtypeStruct((n, LANE), F32),
                   jax.ShapeDtypeStruct((nt, N_EXPERTS, LANE), F32)],
        compiler_params=_params(("parallel",)),
        name="router",
    )(x, wr_t, br, before)


def _start_segments(pc_ref, off_ref, tile, make_copy):
    loc = jnp.int32(0)
    for e in range(N_EXPERTS):
        rows = pc_ref[tile * N_EXPERTS + e]
        glob = off_ref[tile * N_EXPERTS + e]
        nbig = rows // SEG_BIG
        for size, first, count in ((SEG_BIG, 0, nbig), (SEG_ALIGN, nbig * SEG_BIG, (rows % SEG_BIG) // SEG_ALIGN)):
            def start(p, carry, loc=loc + first, glob=glob + first, size=size):
                make_copy(pl.multiple_of(loc + p * size, SEG_ALIGN),
                          pl.multiple_of(glob + p * size, SEG_ALIGN), size).start()
                return carry

            lax.fori_loop(0, count, start, 0)
        loc = loc + rows


def _wait_segments(pc_ref, tile, make_copy):
    nbig = nsmall = jnp.int32(0)
    for e in range(N_EXPERTS):
        rows = pc_ref[tile * N_EXPERTS + e]
        nbig = nbig + rows // SEG_BIG
        nsmall = nsmall + (rows % SEG_BIG) // SEG_ALIGN
    for size, count in ((SEG_BIG, nbig), (SEG_ALIGN, nsmall)):
        lax.fori_loop(0, count, lambda p, carry, size=size: (make_copy(0, 0, size).wait(), carry)[1], 0)


def _dispatch_kernel(pc_ref, off_ref, tail_ref, nused_ref, x_ref, route_ref, buf_ref, stage_ref, zero_ref, sem_ref,
                     *, nblocks):
    i = pl.program_id(0)
    last = pl.num_programs(0) - 1
    slot = i % 2
    d1 = route_ref[R_D1:R_D1 + 1, :].astype(jnp.int32)
    d2 = route_ref[R_D2:R_D2 + 1, :].astype(jnp.int32)
    rows = lax.broadcasted_iota(jnp.int32, (MOE_RTILE, MOE_TT), 0)
    onehot = ((rows == d1) | (rows == d2)).astype(BF16)
    stage_ref[slot] = _dot(onehot, x_ref[...].astype(BF16)).astype(BF16)

    def seg_copy(s):
        return lambda loc, glob, size: pltpu.make_async_copy(
            stage_ref.at[s, pl.ds(loc, size), :], buf_ref.at[pl.ds(glob, size), :], sem_ref.at[s])

    _start_segments(pc_ref, off_ref, i, seg_copy(slot))

    @pl.when(i > 0)
    def _():
        _wait_segments(pc_ref, i - 1, seg_copy(1 - slot))

    @pl.when(i == last)
    def _():
        _wait_segments(pc_ref, i, seg_copy(slot))
        zero_ref[...] = jnp.zeros(zero_ref.shape, BF16)

        def piece(glob):
            return pltpu.make_async_copy(zero_ref.at[pl.ds(0, SEG_ALIGN), :],
                                         buf_ref.at[pl.ds(glob, SEG_ALIGN), :], sem_ref.at[2])

        def block(row):
            return pltpu.make_async_copy(zero_ref, buf_ref.at[pl.ds(row, MOE_RB), :], sem_ref.at[2])

        npieces = jnp.int32(0)
        for e in range(N_EXPERTS):
            start, cnt = tail_ref[e], tail_ref[N_EXPERTS + e]
            lax.fori_loop(0, cnt, lambda p, c, start=start: (
                piece(pl.multiple_of(start + p * SEG_ALIGN, SEG_ALIGN)).start(), c)[1], 0)
            npieces = npieces + cnt
        lax.fori_loop(0, npieces, lambda p, c: (piece(0).wait(), c)[1], 0)
        lax.fori_loop(nused_ref[0], nblocks,
                      lambda b, c: (block(pl.multiple_of(b * MOE_RB, MOE_RB)).start(), c)[1], 0)
        lax.fori_loop(nused_ref[0], nblocks, lambda b, c: (block(0).wait(), c)[1], 0)


def _dispatch(x, route, pc, off, tail, nused, rmax):
    n, d = x.shape
    return pl.pallas_call(
        functools.partial(_dispatch_kernel, nblocks=rmax // MOE_RB),
        grid_spec=pltpu.PrefetchScalarGridSpec(
            num_scalar_prefetch=4,
            grid=(n // MOE_TT,),
            in_specs=[pl.BlockSpec((MOE_TT, d), lambda i, *_: (i, 0)),
                      pl.BlockSpec((N_EXPERTS, MOE_TT), lambda i, *_: (0, i))],
            out_specs=pl.BlockSpec(memory_space=pl.ANY),
            scratch_shapes=[pltpu.VMEM((2, MOE_RTILE, d), BF16), pltpu.VMEM((MOE_RB, d), BF16),
                            pltpu.SemaphoreType.DMA((3,))]),
        out_shape=jax.ShapeDtypeStruct((rmax, d), BF16),
        compiler_params=_params(("arbitrary",)),
        name="dispatch",
    )(pc, off, tail, nused, x, route)


def _experts_kernel(be_ref, nused_ref, x_ref, wg_ref, wu_ref, wd_ref, o_ref):
    @pl.when(pl.program_id(0) < nused_ref[0])
    def _():
        o_ref[...] = _swiglu(x_ref[...], wg_ref, wu_ref, wd_ref, MOE_FF_CHUNKS).astype(BF16)

    @pl.when(pl.program_id(0) >= nused_ref[0])
    def _():
        o_ref[...] = jnp.zeros(o_ref.shape, BF16)


def _experts(buf, be, nused, wgu, wd, *, moe_layer):
    rmax, d = buf.shape
    ff = wd.shape[2]
    blk = pl.BlockSpec((MOE_RB, d), lambda i, be, nu: (jnp.maximum(jnp.minimum(i, nu[0] - 1), 0), 0))
    out_blk = pl.BlockSpec((MOE_RB, d), lambda i, be, nu: (i, 0))
    once = pl.Buffered(1)
    return pl.pallas_call(
        _experts_kernel,
        grid_spec=pltpu.PrefetchScalarGridSpec(
            num_scalar_prefetch=2,
            grid=(rmax // MOE_RB,),
            in_specs=[blk,
                      pl.BlockSpec((None, None, d, ff), lambda i, be, nu: (moe_layer, be[i], 0, 0), pipeline_mode=once),
                      pl.BlockSpec((None, None, d, ff), lambda i, be, nu: (moe_layer, be[i], 0, 1), pipeline_mode=once),
                      pl.BlockSpec((None, None, ff, d), lambda i, be, nu: (moe_layer, be[i], 0, 0), pipeline_mode=once)],
            out_specs=out_blk),
        out_shape=jax.ShapeDtypeStruct((rmax, d), BF16),
        compiler_params=_params(("arbitrary",)),
        name="experts",
    )(be, nused, buf, wgu, wgu, wd)


def _combine_kernel(pc_ref, off_ref, x_ref, routet_ref, g_ref, b_ref, buf_ref, o_ref, stage_ref, sem_ref):
    i = pl.program_id(0)
    last = pl.num_programs(0) - 1
    slot = i % 2

    def seg_copy(s):
        return lambda loc, glob, size: pltpu.make_async_copy(
            buf_ref.at[pl.ds(glob, size), :], stage_ref.at[s, pl.ds(loc, size), :], sem_ref.at[s])

    @pl.when(i == 0)
    def _():
        stage_ref[...] = jnp.zeros(stage_ref.shape, BF16)
        _start_segments(pc_ref, off_ref, i, seg_copy(slot))

    _wait_segments(pc_ref, i, seg_copy(slot))

    @pl.when(i < last)
    def _():
        _start_segments(pc_ref, off_ref, i + 1, seg_copy(1 - slot))

    rt = routet_ref[...]
    col = lambda r: rt[:, r:r + 1]
    d1 = col(R_D1).astype(jnp.int32)
    d2 = col(R_D2).astype(jnp.int32)
    lanes = lax.broadcasted_iota(jnp.int32, (MOE_TT, MOE_RTILE), 1)
    y = stage_ref[slot]
    select = jnp.where(lanes == d1, col(R_W1), jnp.where(lanes == d2, col(R_W2), 0.0))
    f = _dot(select.astype(BF16), y)
    o_ref[...] = _layer_norm(DN_ALPHA * x_ref[...] + f, g_ref[...], b_ref[...])


def _combine(x, routet, pc, off, buf, g, b, *, layer):
    n, d = x.shape
    row = pl.BlockSpec((MOE_TT, d), lambda i, *_: (i, 0))
    return pl.pallas_call(
        _combine_kernel,
        grid_spec=pltpu.PrefetchScalarGridSpec(
            num_scalar_prefetch=2,
            grid=(n // MOE_TT,),
            in_specs=[row, pl.BlockSpec((MOE_TT, LANE), lambda i, *_: (i, 0)),
                      _layer_spec(g, layer), _layer_spec(b, layer), pl.BlockSpec(memory_space=pl.ANY)],
            out_specs=row,
            scratch_shapes=[pltpu.VMEM((2, MOE_RTILE, d), BF16), pltpu.SemaphoreType.DMA((2,))]),
        out_shape=jax.ShapeDtypeStruct((n, d), F32),
        compiler_params=_params(("arbitrary",)),
        name="combine",
    )(pc, off, x, routet, g, b, buf)


def _moe(x, wr_t, b_router, wgu, wd, g, b, *, layer, moe_layer):
    n, d = x.shape
    nt = n // MOE_TT
    route, routet, cnt = _router(x, wr_t, b_router, layer=moe_layer, tm=MOE_TT)

    cnt = cnt[:, :, 0].astype(jnp.int32)
    pc = (cnt + SEG_ALIGN - 1) // SEG_ALIGN * SEG_ALIGN
    total = jnp.sum(pc, axis=0)
    region = (total + MOE_RB - 1) // MOE_RB * MOE_RB
    base = jnp.cumsum(region) - region
    off = base[None, :] + jnp.cumsum(pc, axis=0) - pc
    rmax = (2 * n + nt * N_EXPERTS * (SEG_ALIGN - 1) + N_EXPERTS * (MOE_RB - 1) + MOE_RB - 1) // MOE_RB * MOE_RB
    blk_end = jnp.cumsum(region // MOE_RB)
    nused = blk_end[-1:].astype(jnp.int32)
    blk = jnp.minimum(jnp.arange(rmax // MOE_RB, dtype=jnp.int32), nused - 1)
    be = jnp.sum(blk[:, None] >= blk_end[None, :], axis=1).astype(jnp.int32)
    tail = jnp.concatenate([base + total, (region - total) // SEG_ALIGN]).astype(jnp.int32)
    pc, off = pc.reshape(-1), off.reshape(-1).astype(jnp.int32)

    buf = _dispatch(x, route, pc, off, tail, nused, rmax)
    y = _experts(buf, be, nused, wgu, wd, moe_layer=moe_layer)
    return _combine(x, routet, pc, off, y, g, b, layer=layer)


def _rotary_tables(seq):
    half = ROT_DIM // 2
    inv_freq = jnp.power(jnp.float32(ROPE_THETA), -jnp.arange(half, dtype=F32) / half)
    ang = jnp.arange(seq, dtype=jnp.int32).astype(F32)[:, None] * inv_freq[None, :]
    cos, sin = jnp.cos(ang), jnp.sin(ang)
    d = jnp.arange(LANE) % HEAD_DIM
    first, second = d < half, (d >= half) & (d < ROT_DIM)
    idx = d % half
    cos_t = jnp.where((first | second)[None, :], cos[:, idx], 1.0)
    sin_a = jnp.where(first[None, :], -sin[:, idx], 0.0)
    sin_b = jnp.where(second[None, :], sin[:, idx], 0.0)
    return cos_t, sin_a, sin_b


def kernel(x, mem, w_in, b_gate, attn_sinks, ssm_a_re, ssm_a_im, ssm_log_dt, ssm_b_re, ssm_b_im, ssm_c_re, ssm_c_im, ssm_d, ssm_w_glu, pool_w, pool_scale, w_br_attn, w_br_ssm, w_br_pool, w_o, ln1_g, ln1_b, xa_wq, xa_wkv, xa_wo, ln2_g, ln2_b, ffn_w_gu, ffn_w_down, moe_w_router, moe_b_router, moe_w_gu, moe_w_down, ln3_g, ln3_b):
    bsz, seq, d = x.shape
    n = bsz * seq
    nmem = mem.shape[1]
    cos_t, sin_a, sin_b = _rotary_tables(seq)
    mem2 = mem.reshape(bsz * nmem, d)
    bf = lambda w: w.astype(BF16)
    vec = lambda v: v.reshape(v.shape[0], 1, v.shape[1])

    w_in_b, pool_w_b, wglu_b = bf(w_in), bf(pool_w), bf(ssm_w_glu)
    merge_stacks = (w_in_b, vec(b_gate), bf(w_br_attn), bf(w_br_ssm), bf(w_br_pool), bf(w_o), vec(ln1_g), vec(ln1_b))
    xattn_stacks = (bf(xa_wq), bf(xa_wo), vec(ln2_g), vec(ln2_b))
    wkv_b = bf(xa_wkv)
    ffn_gu_b, ffn_down_b = bf(ffn_w_gu), bf(ffn_w_down)
    moe_gu_b, moe_down_b = bf(moe_w_gu), bf(moe_w_down)
    router_t = jnp.swapaxes(moe_w_router, 1, 2)
    router_b = moe_b_router.reshape(moe_b_router.shape[0], N_EXPERTS, 1)
    pool_scale3, ssm_d3, ln3_g3, ln3_b3 = vec(pool_scale), vec(ssm_d), vec(ln3_g), vec(ln3_b)

    for i in range(DEPTH):
        q, k, v, u, pm = _inproj(x, w_in_b, cos_t, sin_a, sin_b, pool_w_b, pool_scale3,
                                 layer=i, tt=min(TT_INPROJ, seq))
        attn = _swa(q, k, v, attn_sinks, layer=i, tq=min(TQ_SWA, seq))
        bw, cw, are, aim = _ssm_weights(ssm_a_re[i], ssm_a_im[i], ssm_log_dt[i], ssm_b_re[i], ssm_b_im[i],
                                        ssm_c_re[i], ssm_c_im[i], bsz)
        hs = _ssm(u, bw, cw, are, aim, ssm_d3, wglu_b, layer=i, tc=TC_SSM)
        x2 = _merge(x.reshape(n, d), attn.reshape(n, -1), hs.reshape(n, -1), pm.reshape(n, -1), merge_stacks,
                    layer=i, tm=TM_MERGE)

        kv = _matmul(mem2, wkv_b, layer=i, tm=512, out_dtype=BF16).reshape(bsz, nmem, 2 * d)
        x3 = _xattn(x2.reshape(bsz, seq, d), kv, xattn_stacks, layer=i, tm=min(TM_XATTN, seq)).reshape(n, d)

        j = i // 2
        if i % 2 == 0:
            x4 = _ffn(x3, ffn_gu_b, ffn_down_b, ln3_g3, ln3_b3, layer=i, ffn_layer=j, tm=TM_FFN, nchunk=FFN_CHUNKS)
        else:
            x4 = _moe(x3, router_t, router_b, moe_gu_b, moe_down_b, ln3_g3, ln3_b3, layer=i, moe_layer=j)
        x = x4.reshape(bsz, seq, d)
    return x
```

```python
import functools

import jax
import jax.numpy as jnp
from jax import lax
from jax.experimental import pallas as pl
from jax.experimental.pallas import tpu as pltpu

D_MODEL = 1024
DEPTH = 4
N_Q_HEADS = 8
N_KV_HEADS = 2
HEAD_DIM = 64
WINDOW = 128
ROT_DIM = HEAD_DIM // 4
ROPE_THETA = 500000.0
SSM_WIDTH = D_MODEL // 2
SSM_GROUP = 16
SSM_GROUPS = SSM_WIDTH // SSM_GROUP
SSM_STATE = 64
SSM_LANES = SSM_GROUPS * SSM_STATE
POOL_WINDOWS = (2, 4, 8, 16)
POOL_WIDTH = D_MODEL // 2
POOL_GROUP = POOL_WIDTH // len(POOL_WINDOWS)
POOL_HALO = 16
X_HEADS = 4
X_HEAD_DIM = D_MODEL // X_HEADS
D_FF = 2816
N_EXPERTS = 8
D_FF_EXPERT = 3584
Q_WIDTH = N_Q_HEADS * HEAD_DIM
KV_WIDTH = N_KV_HEADS * HEAD_DIM
QKV_WIDTH = Q_WIDTH + 2 * KV_WIDTH
BRANCH_IN_WIDTH = QKV_WIDTH + SSM_WIDTH + POOL_WIDTH
DN_ALPHA = (2.0 * DEPTH) ** 0.25
LN_EPS = 1e-5
NEG_INF = -1e30

TT_INPROJ = 1024
TQ_SWA = 1024
TC_SSM = 128
TM_MERGE = 1024
TM_XATTN = 1024
TM_FFN = 512
FFN_CHUNKS = 2

LANE = 128
MXU_TILE = 256
VMEM_LIMIT = 56 * 1024 * 1024

BF16 = jnp.bfloat16
F32 = jnp.float32


def _dot(a, b):
    return jnp.dot(a, b, preferred_element_type=F32)


def _dot_nt(a, b):
    return lax.dot_general(a, b, (((1,), (1,)), ((), ())), preferred_element_type=F32)


def _layer_norm(y, g, b):
    mu = jnp.mean(y, axis=-1, keepdims=True)
    d = y - mu
    var = jnp.mean(d * d, axis=-1, keepdims=True)
    return d * lax.rsqrt(var + LN_EPS) * g + b


def _const_spec(shape):
    nd = len(shape)
    return pl.BlockSpec(shape, lambda *_: (0,) * nd, pipeline_mode=pl.Buffered(1))


def _layer_spec(stack, layer, width=None, col=0):
    rest = stack.shape[1:]
    block = (None,) + rest[:-1] + (rest[-1] if width is None else width,)
    index = (layer,) + (0,) * (len(rest) - 1) + (col,)
    return pl.BlockSpec(block, lambda *_: index, pipeline_mode=pl.Buffered(1))


def _params(sem, limit=VMEM_LIMIT):
    return pltpu.CompilerParams(dimension_semantics=sem, vmem_limit_bytes=limit)


def _rotary(t, cos, sin_a, sin_b):
    return t * cos + pltpu.roll(t, LANE - ROT_DIM // 2, 1) * sin_a + pltpu.roll(t, ROT_DIM // 2, 1) * sin_b


def _inproj_kernel(x_ref, w_ref, cos_ref, sa_ref, sb_ref, pw_ref, ps_ref,
                   q_ref, k_ref, v_ref, u_ref, pm_ref, hist_ref, *, tt):
    ti = pl.program_id(1)
    xb = x_ref[...].astype(BF16)
    cos, sa, sb = cos_ref[...], sa_ref[...], sb_ref[...]

    qkv = _dot(xb, w_ref[:, 0:QKV_WIDTH])
    for c in range(Q_WIDTH // LANE):
        t = qkv[:, c * LANE:(c + 1) * LANE] * (HEAD_DIM ** -0.5)
        q_ref[:, c * LANE:(c + 1) * LANE] = _rotary(t, cos, sa, sb).astype(BF16)
    k_ref[...] = _rotary(qkv[:, Q_WIDTH:Q_WIDTH + KV_WIDTH], cos, sa, sb).astype(BF16)
    v_ref[...] = qkv[:, Q_WIDTH + KV_WIDTH:].astype(BF16)

    u_ref[...] = _dot(xb, w_ref[:, QKV_WIDTH:QKV_WIDTH + SSM_WIDTH])

    @pl.when(ti == 0)
    def _():
        hist_ref[0:POOL_HALO, :] = jnp.zeros((POOL_HALO, POOL_WIDTH), F32)

    up = _dot(xb, w_ref[:, QKV_WIDTH + SSM_WIDTH:BRANCH_IN_WIDTH])
    hist_ref[POOL_HALO:POOL_HALO + tt, :] = up
    pos1 = (ti * tt + 1 + lax.broadcasted_iota(jnp.int32, (tt, 1), 0)).astype(F32)
    for gi, w in enumerate(POOL_WINDOWS):
        lanes = slice(gi * POOL_GROUP, (gi + 1) * POOL_GROUP)
        acc = hist_ref[:, lanes]
        k = 1
        while k < w:
            acc = acc + pltpu.roll(acc, k, 0)
            k *= 2
        pooled = acc[POOL_HALO:, :] / jnp.minimum(pos1, float(w)) - up[:, lanes]
        mixed = _dot(pooled.astype(BF16), pw_ref[gi]) * ps_ref[:, lanes]
        pm_ref[:, lanes] = mixed.astype(BF16)
    hist_ref[0:POOL_HALO, :] = hist_ref[tt:tt + POOL_HALO, :]


def _inproj(x, w_in, cos, sa, sb, pool_w, pool_scale, *, layer, tt):
    bsz, seq, d = x.shape
    row = lambda w: pl.BlockSpec((None, tt, w), lambda b, t: (b, t, 0))
    tab = pl.BlockSpec((tt, LANE), lambda b, t: (t, 0))
    return pl.pallas_call(
        functools.partial(_inproj_kernel, tt=tt),
        grid=(bsz, seq // tt),
        in_specs=[row(d), _layer_spec(w_in, layer, width=BRANCH_IN_WIDTH), tab, tab, tab,
                  _layer_spec(pool_w, layer), _layer_spec(pool_scale, layer)],
        out_specs=[row(Q_WIDTH), row(KV_WIDTH), row(KV_WIDTH), row(SSM_WIDTH), row(POOL_WIDTH)],
        out_shape=[jax.ShapeDtypeStruct((bsz, seq, Q_WIDTH), BF16),
                   jax.ShapeDtypeStruct((bsz, seq, KV_WIDTH), BF16),
                   jax.ShapeDtypeStruct((bsz, seq, KV_WIDTH), BF16),
                   jax.ShapeDtypeStruct((bsz, seq, SSM_WIDTH), F32),
                   jax.ShapeDtypeStruct((bsz, seq, POOL_WIDTH), BF16)],
        scratch_shapes=[pltpu.VMEM((tt + POOL_HALO, POOL_WIDTH), F32)],
        compiler_params=_params(("arbitrary", "arbitrary")),
        name="inproj",
    )(x, w_in, cos, sa, sb, pool_w, pool_scale)


def _half_lanes(a, b, lo):
    return jnp.where(lo, a, b)


def _swa_kernel(sink_ref, q_ref, k_ref, kp_ref, v_ref, vp_ref, o_ref, *, nsub):
    ti = pl.program_id(1)
    blk = WINDOW
    lo = lax.broadcasted_iota(jnp.int32, (2 * blk, LANE), 1) < HEAD_DIM
    tri = (lax.broadcasted_iota(jnp.int32, (2 * blk, blk), 1)
           <= lax.broadcasted_iota(jnp.int32, (2 * blk, blk), 0) % blk)
    top = lax.broadcasted_iota(jnp.int32, (2 * blk, 1), 0) < blk
    zero = jnp.zeros((2 * blk, LANE), F32)
    ones_sel = ((lax.broadcasted_iota(jnp.int32, (4 * blk, LANE), 0) < 2 * blk)
                == (lax.broadcasted_iota(jnp.int32, (4 * blk, LANE), 1) < HEAD_DIM)).astype(BF16)

    for j in range(nsub):
        if j == 0:
            kc = jnp.concatenate([kp_ref[...], k_ref[0:blk, :]], axis=0)
            vc = jnp.concatenate([vp_ref[...], v_ref[0:blk, :]], axis=0)
        else:
            kc = k_ref[(j - 1) * blk:(j + 1) * blk, :]
            vc = v_ref[(j - 1) * blk:(j + 1) * blk, :]
        kc = kc.astype(F32)
        vc = vc.astype(F32)
        kr = pltpu.roll(kc, HEAD_DIM, 1)
        vr = pltpu.roll(vc, HEAD_DIM, 1)
        kmat = [[_half_lanes(kc, zero, lo), _half_lanes(zero, kr, lo)],
                [_half_lanes(kr, zero, lo), _half_lanes(zero, kc, lo)]]
        vmat = [[_half_lanes(vc, zero, lo), _half_lanes(zero, vr, lo)],
                [_half_lanes(vr, zero, lo), _half_lanes(zero, vc, lo)]]
        rows = pl.ds(j * blk, blk)
        for kh in range(N_KV_HEADS):
            base = kh * 2 * LANE
            qq = jnp.concatenate([q_ref[rows, base:base + LANE], q_ref[rows, base + LANE:base + 2 * LANE]], axis=0)
            probs, sink_term = [], []
            for par in range(2):
                s = _dot_nt(qq, kmat[kh][par].astype(BF16))
                s_prev = s[:, :blk] if j > 0 else jnp.where(ti > 0, s[:, :blk], NEG_INF)
                s = jnp.where(tri, s[:, blk:], s_prev)
                h_top = kh * 4 + par
                sink = jnp.where(top, sink_ref[h_top], sink_ref[h_top + 2])
                m = jnp.maximum(jnp.max(s, axis=-1, keepdims=True), sink)
                p = jnp.exp(s - m)
                sink_term.append(jnp.exp(sink - m))
                probs.append(jnp.concatenate([jnp.where(tri, 0.0, p), jnp.where(tri, p, 0.0)], axis=1).astype(BF16))
            pcat = jnp.concatenate(probs, axis=1)
            vcat = jnp.concatenate([vmat[kh][0], vmat[kh][1]], axis=0).astype(BF16)
            ov = _dot(pcat, jnp.concatenate([vcat, ones_sel], axis=1))
            den = ov[:, LANE:] + jnp.where(lo, sink_term[0], sink_term[1])
            o = ov[:, :LANE] / den
            o_ref[rows, base:base + LANE] = o[0:blk].astype(BF16)
            o_ref[rows, base + LANE:base + 2 * LANE] = o[blk:2 * blk].astype(BF16)


def _swa(q, k, v, sinks, *, layer, tq):
    bsz, seq, _ = q.shape
    nsub = tq // WINDOW
    cur = lambda w: pl.BlockSpec((None, tq, w), lambda b, t: (b, t, 0))
    prev = pl.BlockSpec((None, WINDOW, KV_WIDTH), lambda b, t: (b, jnp.maximum(t * nsub - 1, 0), 0))
    return pl.pallas_call(
        functools.partial(_swa_kernel, nsub=nsub),
        grid=(bsz, seq // tq),
        in_specs=[pl.BlockSpec(memory_space=pltpu.SMEM), cur(Q_WIDTH), cur(KV_WIDTH), prev, cur(KV_WIDTH), prev],
        out_specs=cur(Q_WIDTH),
        out_shape=jax.ShapeDtypeStruct((bsz, seq, Q_WIDTH), BF16),
        compiler_params=_params(("parallel", "parallel")),
        name="swa",
    )(sinks[layer], q, k, k, v, v)


SSM_CHUNKS = 4
SSM_CHUNK_LANES = SSM_LANES // SSM_CHUNKS
SSM_CHUNK_IN = SSM_WIDTH // SSM_CHUNKS
SCAN_LANES = 1024


def _ssm_kernel(u_ref, bw_ref, cw_ref, are_ref, aim_ref, d_ref, wglu_ref, o_ref, xs_ref, st_ref, h_ref,
                *, tc, bsz):
    @pl.when(pl.program_id(0) == 0)
    def _():
        h_ref[...] = jnp.zeros(h_ref.shape, F32)

    for b in range(bsz):
        for c in range(SSM_CHUNKS):
            xs_ref[c, pl.ds(b, tc, stride=bsz), :] = u_ref[b, :, c * SSM_CHUNK_IN:(c + 1) * SSM_CHUNK_IN]

    for part in range(2):
        for c in range(SSM_CHUNKS):
            col = part * SSM_LANES + c * SSM_CHUNK_LANES
            st_ref[:, col:col + SSM_CHUNK_LANES] = _dot(xs_ref[c].astype(BF16), bw_ref[part * SSM_CHUNKS + c])

    for jb in range(SSM_LANES // SCAN_LANES):
        re = pl.ds(jb * SCAN_LANES, SCAN_LANES)
        im = pl.ds(SSM_LANES + jb * SCAN_LANES, SCAN_LANES)
        ar = are_ref[:, re]
        ai = aim_ref[:, re]

        def step(t, carry):
            hr, hi = carry
            r = pl.ds(pl.multiple_of(t * bsz, bsz), bsz)
            nr = ar * hr - ai * hi + st_ref[r, re]
            ni = ar * hi + ai * hr + st_ref[r, im]
            st_ref[r, re] = nr
            st_ref[r, im] = ni
            return nr, ni

        hr, hi = lax.fori_loop(0, tc, step, (h_ref[:, re], h_ref[:, im]), unroll=8)
        h_ref[:, re] = hr
        h_ref[:, im] = hi

    ys = []
    for c in range(SSM_CHUNKS):
        sre = st_ref[:, c * SSM_CHUNK_LANES:(c + 1) * SSM_CHUNK_LANES].astype(BF16)
        sim = st_ref[:, SSM_LANES + c * SSM_CHUNK_LANES:SSM_LANES + (c + 1) * SSM_CHUNK_LANES].astype(BF16)
        cols = slice(c * SSM_CHUNK_IN, (c + 1) * SSM_CHUNK_IN)
        y = _dot(sre, cw_ref[c]) + _dot(sim, cw_ref[SSM_CHUNKS + c])
        ys.append(y + d_ref[:, cols] * xs_ref[c])

    hh = jax.nn.gelu(jnp.concatenate(ys, axis=1)).astype(BF16)
    z = _dot(hh, wglu_ref[...])
    res = z[:, :SSM_WIDTH] * jax.nn.sigmoid(z[:, SSM_WIDTH:])
    for c in range(SSM_CHUNKS):
        xs_ref[c] = res[:, c * SSM_CHUNK_IN:(c + 1) * SSM_CHUNK_IN]
    for b in range(bsz):
        for c in range(SSM_CHUNKS):
            o_ref[b, :, c * SSM_CHUNK_IN:(c + 1) * SSM_CHUNK_IN] = (
                xs_ref[c, pl.ds(b, tc, stride=bsz), :].astype(BF16))


def _ssm(u, bw, cw, are, aim, d_skip, wglu, *, layer, tc):
    bsz, seq, _ = u.shape
    blk = pl.BlockSpec((bsz, tc, SSM_WIDTH), lambda t: (0, t, 0))
    return pl.pallas_call(
        functools.partial(_ssm_kernel, tc=tc, bsz=bsz),
        grid=(seq // tc,),
        in_specs=[blk, _const_spec(bw.shape), _const_spec(cw.shape), _const_spec(are.shape),
                  _const_spec(aim.shape), _layer_spec(d_skip, layer), _layer_spec(wglu, layer)],
        out_specs=blk,
        out_shape=jax.ShapeDtypeStruct((bsz, seq, SSM_WIDTH), BF16),
        scratch_shapes=[pltpu.VMEM((SSM_CHUNKS, tc * bsz, SSM_CHUNK_IN), F32),
                        pltpu.VMEM((tc * bsz, 2 * SSM_LANES), F32),
                        pltpu.VMEM((bsz, 2 * SSM_LANES), F32)],
        compiler_params=_params(("arbitrary",)),
        name="ssm",
    )(u, bw, cw, are, aim, d_skip, wglu)


def _ssm_weights(a_re, a_im, log_dt, b_re, b_im, c_re, c_im, bsz):
    dt = jnp.exp(log_dt)[:, None]
    decay = jnp.exp(dt * a_re)
    abar_re, abar_im = decay * jnp.cos(dt * a_im), decay * jnp.sin(dt * a_im)
    inv_abs2 = 1.0 / (a_re * a_re + a_im * a_im)
    num_re, num_im = abar_re - 1.0, abar_im
    f_re = (num_re * a_re + num_im * a_im) * inv_abs2
    f_im = (num_im * a_re - num_re * a_im) * inv_abs2
    bbar_re = f_re[..., None] * b_re - f_im[..., None] * b_im
    bbar_im = f_re[..., None] * b_im + f_im[..., None] * b_re
    gpc = SSM_GROUPS // SSM_CHUNKS
    eye = jnp.eye(gpc, dtype=F32)

    def in_chunks(bb):
        t = bb.reshape(SSM_CHUNKS, gpc, SSM_STATE, SSM_GROUP)
        return jnp.einsum('cgnp,gh->cgphn', t, eye).reshape(SSM_CHUNKS, gpc * SSM_GROUP, gpc * SSM_STATE)

    def out_chunks(cc):
        t = cc.reshape(SSM_CHUNKS, gpc, SSM_GROUP, SSM_STATE)
        return jnp.einsum('cgpn,gh->cgnhp', t, eye).reshape(SSM_CHUNKS, gpc * SSM_STATE, gpc * SSM_GROUP)

    bw = jnp.concatenate([in_chunks(bbar_re), in_chunks(bbar_im)], axis=0).astype(BF16)
    cw = jnp.concatenate([out_chunks(c_re), out_chunks(-c_im)], axis=0).astype(BF16)
    are = jnp.broadcast_to(abar_re.reshape(1, SSM_LANES), (bsz, SSM_LANES))
    aim = jnp.broadcast_to(abar_im.reshape(1, SSM_LANES), (bsz, SSM_LANES))
    return bw, cw, are, aim


def _merge_kernel(x_ref, a_ref, s_ref, p_ref, win_ref, bg_ref, wa_ref, ws_ref, wp_ref, wo_ref, g_ref, b_ref, o_ref):
    x = x_ref[...]
    xb = x.astype(BF16)
    merged = None
    for i, (br_ref, w_ref) in enumerate(((a_ref, wa_ref), (s_ref, ws_ref), (p_ref, wp_ref))):
        cols = slice(i * D_MODEL, (i + 1) * D_MODEL)
        wcols = slice(BRANCH_IN_WIDTH + i * D_MODEL, BRANCH_IN_WIDTH + (i + 1) * D_MODEL)
        gate = jax.nn.sigmoid(_dot(xb, win_ref[:, wcols]) + bg_ref[:, cols])
        term = gate * _dot(br_ref[...], w_ref[...])
        merged = term if merged is None else merged + term
    h = _dot(merged.astype(BF16), wo_ref[...])
    o_ref[...] = _layer_norm(DN_ALPHA * x + h, g_ref[...], b_ref[...])


def _merge(x, a, s, p, stacks, *, layer, tm):
    n, d = x.shape
    row = lambda w: pl.BlockSpec((tm, w), lambda i: (i, 0))
    return pl.pallas_call(
        _merge_kernel,
        grid=(n // tm,),
        in_specs=[row(d), row(Q_WIDTH), row(SSM_WIDTH), row(POOL_WIDTH)] + [_layer_spec(c, layer) for c in stacks],
        out_specs=row(d),
        out_shape=jax.ShapeDtypeStruct((n, d), F32),
        compiler_params=_params(("parallel",)),
        name="merge",
    )(x, a, s, p, *stacks)


def _matmul_kernel(a_ref, w_ref, o_ref):
    o_ref[...] = _dot(a_ref[...].astype(BF16), w_ref[...]).astype(o_ref.dtype)


def _matmul(a, w, *, layer, tm, out_dtype):
    m, k = a.shape
    n = w.shape[-1]
    return pl.pallas_call(
        _matmul_kernel,
        grid=(m // tm,),
        in_specs=[pl.BlockSpec((tm, k), lambda i: (i, 0)), _layer_spec(w, layer)],
        out_specs=pl.BlockSpec((tm, n), lambda i: (i, 0)),
        out_shape=jax.ShapeDtypeStruct((m, n), out_dtype),
        compiler_params=_params(("parallel",)),
        name="kvproj",
    )(a, w)


def _xattn_kernel(x_ref, k_ref, v_ref, wq_ref, wo_ref, g_ref, b_ref, o_ref):
    x = x_ref[...]
    q = (_dot(x.astype(BF16), wq_ref[...]) * (X_HEAD_DIM ** -0.5)).astype(BF16)
    outs = []
    for h in range(X_HEADS):
        cols = slice(h * X_HEAD_DIM, (h + 1) * X_HEAD_DIM)
        s = _dot_nt(q[:, cols], k_ref[:, cols])
        p = jnp.exp(s - jnp.max(s, axis=-1, keepdims=True))
        inv = 1.0 / jnp.sum(p, axis=-1, keepdims=True)
        outs.append((_dot(p.astype(BF16), v_ref[:, cols]) * inv).astype(BF16))
    c = _dot(jnp.concatenate(outs, axis=1), wo_ref[...])
    o_ref[...] = _layer_norm(DN_ALPHA * x + c, g_ref[...], b_ref[...])


def _xattn(x, kv, stacks, *, layer, tm):
    bsz, seq, d = x.shape
    nmem = kv.shape[1]
    row = pl.BlockSpec((None, tm, d), lambda bi, t: (bi, t, 0))
    kspec = pl.BlockSpec((None, nmem, d), lambda bi, t: (bi, 0, 0))
    vspec = pl.BlockSpec((None, nmem, d), lambda bi, t: (bi, 0, 1))
    return pl.pallas_call(
        _xattn_kernel,
        grid=(bsz, seq // tm),
        in_specs=[row, kspec, vspec] + [_layer_spec(c, layer) for c in stacks],
        out_specs=row,
        out_shape=jax.ShapeDtypeStruct((bsz, seq, d), F32),
        compiler_params=_params(("parallel", "parallel")),
        name="xattn",
    )(x, kv, kv, *stacks)


def _ff_chunks(width, nchunk):
    tiles = width // MXU_TILE
    assert tiles * MXU_TILE == width
    bounds = [MXU_TILE * ((tiles * c) // nchunk) for c in range(nchunk + 1)]
    return [slice(lo, hi) for lo, hi in zip(bounds[:-1], bounds[1:])]


def _swiglu(xb, wg_ref, wu_ref, wd_ref, nchunk):
    acc = None
    for cols in _ff_chunks(wg_ref.shape[1], nchunk):
        hid = jax.nn.silu(_dot(xb, wg_ref[:, cols])) * _dot(xb, wu_ref[:, cols])
        part = _dot(hid.astype(BF16), wd_ref[cols, :])
        acc = part if acc is None else acc + part
    return acc


def _ffn_kernel(x_ref, wg_ref, wu_ref, wd_ref, g_ref, b_ref, egu_ref, edn_ref, o_ref, egu_out_ref, edn_out_ref,
                *, nchunk):
    x = x_ref[...]
    acc = _swiglu(x.astype(BF16), wg_ref, wu_ref, wd_ref, nchunk)
    o_ref[...] = _layer_norm(DN_ALPHA * x + acc, g_ref[...], b_ref[...])
    egu_out_ref[...] = egu_ref[...].astype(BF16)
    edn_out_ref[...] = edn_ref[...].astype(BF16)


def _ffn(x, wgu, wd, g, b, expert_gu, expert_down, *, layer, ffn_layer, tm, nchunk):
    n, d = x.shape
    ff = wd.shape[1]
    steps = n // tm
    _, ne, gu_rows, gu_cols = expert_gu.shape
    _, _, dn_rows, dn_cols = expert_down.shape
    per_expert = steps // ne
    assert per_expert * ne == steps

    def slices(rows, cols):
        blk = (None, None, rows // per_expert, cols)
        return (pl.BlockSpec(blk, lambda i: (ffn_layer, i // per_expert, i % per_expert, 0)),
                pl.BlockSpec(blk, lambda i: (0, i // per_expert, i % per_expert, 0)))

    gu_in, gu_out = slices(gu_rows, gu_cols)
    dn_in, dn_out = slices(dn_rows, dn_cols)
    row = pl.BlockSpec((tm, d), lambda i: (i, 0))
    return pl.pallas_call(
        functools.partial(_ffn_kernel, nchunk=nchunk),
        grid=(steps,),
        in_specs=[row, _layer_spec(wgu, ffn_layer, width=ff, col=0), _layer_spec(wgu, ffn_layer, width=ff, col=1),
                  _layer_spec(wd, ffn_layer), _layer_spec(g, layer), _layer_spec(b, layer), gu_in, dn_in],
        out_specs=[row, gu_out, dn_out],
        out_shape=[jax.ShapeDtypeStruct((n, d), F32),
                   jax.ShapeDtypeStruct((1,) + expert_gu.shape[1:], BF16),
                   jax.ShapeDtypeStruct((1,) + expert_down.shape[1:], BF16)],
        compiler_params=_params(("parallel",)),
        name="ffn",
    )(x, wgu, wgu, wd, g, b, expert_gu, expert_down)


MOE_TT = 512
SEG_ALIGN = 16
SEG_BIG = 128
MOE_RB = 512
MOE_RTILE = 2 * MOE_TT + N_EXPERTS * SEG_ALIGN
MOE_FF_CHUNKS = 2
R_D1, R_D2, R_W1, R_W2 = range(4)


def _router_kernel(x_ref, wr_ref, br_ref, before_ref, route_ref, routet_ref, cnt_ref, *, tm):
    x, wr = x_ref[...], wr_ref[...]
    x_hi, wr_hi = x.astype(BF16), wr.astype(BF16)
    x_lo, wr_lo = (x - x_hi.astype(F32)).astype(BF16), (wr - wr_hi.astype(F32)).astype(BF16)
    logits = _dot_nt(wr_hi, x_hi) + _dot_nt(wr_hi, x_lo) + _dot_nt(wr_lo, x_hi) + br_ref[...]
    eid = lax.broadcasted_iota(jnp.int32, logits.shape, 0)
    v1 = jnp.max(logits, axis=0, keepdims=True)
    e1 = jnp.min(jnp.where(logits == v1, eid, N_EXPERTS), axis=0, keepdims=True)
    rest = jnp.where(eid == e1, -jnp.inf, logits)
    v2 = jnp.max(rest, axis=0, keepdims=True)
    e2 = jnp.min(jnp.where(rest == v2, eid, N_EXPERTS), axis=0, keepdims=True)
    t = jnp.exp(v2 - v1)
    w1 = 1.0 / (1.0 + t)
    w2 = t / (1.0 + t)
    sel1, sel2 = eid == e1, eid == e2
    member = (sel1 | sel2).astype(BF16)
    rank = _dot(member, before_ref[...])
    cnt = jnp.sum(member.astype(F32), axis=1, keepdims=True)
    cnt_ref[...] = jnp.broadcast_to(cnt, (N_EXPERTS, LANE))
    padded_cnt = jnp.floor((cnt + (SEG_ALIGN - 1)) * (1.0 / SEG_ALIGN)) * SEG_ALIGN
    seg = jnp.broadcast_to(padded_cnt, logits.shape)
    d1 = jnp.sum(jnp.where(eid < e1, seg, 0.0) + jnp.where(sel1, rank, 0.0), axis=0, keepdims=True)
    d2 = jnp.sum(jnp.where(eid < e2, seg, 0.0) + jnp.where(sel2, rank, 0.0), axis=0, keepdims=True)
    route = jnp.zeros(logits.shape, F32)
    for r, val in ((R_D1, d1), (R_D2, d2), (R_W1, w1), (R_W2, w2)):
        route = jnp.where(eid == r, val, route)
    route_ref[...] = route
    padded = jnp.concatenate([route, jnp.zeros((LANE - N_EXPERTS, tm), F32)], axis=0)
    routet_ref[...] = padded.T


def _router(x, wr_t, br, *, layer, tm):
    n, d = x.shape
    nt = n // tm
    before = (jnp.arange(tm)[:, None] < jnp.arange(tm)[None, :]).astype(BF16)
    return pl.pallas_call(
        functools.partial(_router_kernel, tm=tm),
        grid=(nt,),
        in_specs=[pl.BlockSpec((tm, d), lambda i: (i, 0)), _layer_spec(wr_t, layer), _layer_spec(br, layer),
                  _const_spec(before.shape)],
        out_specs=[pl.BlockSpec((N_EXPERTS, tm), lambda i: (0, i)),
                   pl.BlockSpec((tm, LANE), lambda i: (i, 0)),
                   pl.BlockSpec((None, N_EXPERTS, LANE), lambda i: (i, 0, 0))],
        out_shape=[jax.ShapeDtypeStruct((N_EXPERTS, n), F32),
                   jax.ShapeDtypeStruct((n, LANE), F32),
                   jax.ShapeDtypeStruct((nt, N_EXPERTS, LANE), F32)],
        compiler_params=_params(("parallel",)),
        name="router",
    )(x, wr_t, br, before)


def _start_segments(pc_ref, off_ref, tile, make_copy):
    loc = jnp.int32(0)
    for e in range(N_EXPERTS):
        rows = pc_ref[tile * N_EXPERTS + e]
        glob = off_ref[tile * N_EXPERTS + e]
        nbig = rows // SEG_BIG
        for size, first, count in ((SEG_BIG, 0, nbig), (SEG_ALIGN, nbig * SEG_BIG, (rows % SEG_BIG) // SEG_ALIGN)):
            def start(p, carry, loc=loc + first, glob=glob + first, size=size):
                make_copy(pl.multiple_of(loc + p * size, SEG_ALIGN),
                          pl.multiple_of(glob + p * size, SEG_ALIGN), size).start()
                return carry

            lax.fori_loop(0, count, start, 0)
        loc = loc + rows


def _wait_segments(pc_ref, tile, make_copy):
    nbig = nsmall = jnp.int32(0)
    for e in range(N_EXPERTS):
        rows = pc_ref[tile * N_EXPERTS + e]
        nbig = nbig + rows // SEG_BIG
        nsmall = nsmall + (rows % SEG_BIG) // SEG_ALIGN
    for size, count in ((SEG_BIG, nbig), (SEG_ALIGN, nsmall)):
        lax.fori_loop(0, count, lambda p, carry, size=size: (make_copy(0, 0, size).wait(), carry)[1], 0)


def _dispatch_kernel(pc_ref, off_ref, tail_ref, nused_ref, x_ref, route_ref, buf_ref, stage_ref, zero_ref, sem_ref,
                     *, nblocks):
    i = pl.program_id(0)
    last = pl.num_programs(0) - 1
    slot = i % 2
    d1 = route_ref[R_D1:R_D1 + 1, :].astype(jnp.int32)
    d2 = route_ref[R_D2:R_D2 + 1, :].astype(jnp.int32)
    rows = lax.broadcasted_iota(jnp.int32, (MOE_RTILE, MOE_TT), 0)
    onehot = ((rows == d1) | (rows == d2)).astype(BF16)
    stage_ref[slot] = _dot(onehot, x_ref[...].astype(BF16)).astype(BF16)

    def seg_copy(s):
        return lambda loc, glob, size: pltpu.make_async_copy(
            stage_ref.at[s, pl.ds(loc, size), :], buf_ref.at[pl.ds(glob, size), :], sem_ref.at[s])

    _start_segments(pc_ref, off_ref, i, seg_copy(slot))

    @pl.when(i > 0)
    def _():
        _wait_segments(pc_ref, i - 1, seg_copy(1 - slot))

    @pl.when(i == last)
    def _():
        _wait_segments(pc_ref, i, seg_copy(slot))
        zero_ref[...] = jnp.zeros(zero_ref.shape, BF16)

        def piece(glob):
            return pltpu.make_async_copy(zero_ref.at[pl.ds(0, SEG_ALIGN), :],
                                         buf_ref.at[pl.ds(glob, SEG_ALIGN), :], sem_ref.at[2])

        def block(row):
            return pltpu.make_async_copy(zero_ref, buf_ref.at[pl.ds(row, MOE_RB), :], sem_ref.at[2])

        npieces = jnp.int32(0)
        for e in range(N_EXPERTS):
            start, cnt = tail_ref[e], tail_ref[N_EXPERTS + e]
            lax.fori_loop(0, cnt, lambda p, c, start=start: (
                piece(pl.multiple_of(start + p * SEG_ALIGN, SEG_ALIGN)).start(), c)[1], 0)
            npieces = npieces + cnt
        lax.fori_loop(0, npieces, lambda p, c: (piece(0).wait(), c)[1], 0)
        lax.fori_loop(nused_ref[0], nblocks,
                      lambda b, c: (block(pl.multiple_of(b * MOE_RB, MOE_RB)).start(), c)[1], 0)
        lax.fori_loop(nused_ref[0], nblocks, lambda b, c: (block(0).wait(), c)[1], 0)


def _dispatch(x, route, pc, off, tail, nused, rmax):
    n, d = x.shape
    return pl.pallas_call(
        functools.partial(_dispatch_kernel, nblocks=rmax // MOE_RB),
        grid_spec=pltpu.PrefetchScalarGridSpec(
            num_scalar_prefetch=4,
            grid=(n // MOE_TT,),
            in_specs=[pl.BlockSpec((MOE_TT, d), lambda i, *_: (i, 0)),
                      pl.BlockSpec((N_EXPERTS, MOE_TT), lambda i, *_: (0, i))],
            out_specs=pl.BlockSpec(memory_space=pl.ANY),
            scratch_shapes=[pltpu.VMEM((2, MOE_RTILE, d), BF16), pltpu.VMEM((MOE_RB, d), BF16),
                            pltpu.SemaphoreType.DMA((3,))]),
        out_shape=jax.ShapeDtypeStruct((rmax, d), BF16),
        compiler_params=_params(("arbitrary",)),
        name="dispatch",
    )(pc, off, tail, nused, x, route)


def _experts_kernel(be_ref, nused_ref, x_ref, wg_ref, wu_ref, wd_ref, o_ref):
    @pl.when(pl.program_id(0) < nused_ref[0])
    def _():
        o_ref[...] = _swiglu(x_ref[...], wg_ref, wu_ref, wd_ref, MOE_FF_CHUNKS).astype(BF16)

    @pl.when(pl.program_id(0) >= nused_ref[0])
    def _():
        o_ref[...] = jnp.zeros(o_ref.shape, BF16)


def _experts(buf, be, nused, wgu, wd, *, moe_layer):
    rmax, d = buf.shape
    ff = wd.shape[2]
    blk = pl.BlockSpec((MOE_RB, d), lambda i, be, nu: (jnp.maximum(jnp.minimum(i, nu[0] - 1), 0), 0))
    out_blk = pl.BlockSpec((MOE_RB, d), lambda i, be, nu: (i, 0))
    once = pl.Buffered(1)
    return pl.pallas_call(
        _experts_kernel,
        grid_spec=pltpu.PrefetchScalarGridSpec(
            num_scalar_prefetch=2,
            grid=(rmax // MOE_RB,),
            in_specs=[blk,
                      pl.BlockSpec((None, None, d, ff), lambda i, be, nu: (moe_layer, be[i], 0, 0), pipeline_mode=once),
                      pl.BlockSpec((None, None, d, ff), lambda i, be, nu: (moe_layer, be[i], 0, 1), pipeline_mode=once),
                      pl.BlockSpec((None, None, ff, d), lambda i, be, nu: (moe_layer, be[i], 0, 0), pipeline_mode=once)],
            out_specs=out_blk),
        out_shape=jax.ShapeDtypeStruct((rmax, d), BF16),
        compiler_params=_params(("arbitrary",)),
        name="experts",
    )(be, nused, buf, wgu, wgu, wd)


def _combine_kernel(pc_ref, off_ref, x_ref, routet_ref, g_ref, b_ref, buf_ref, o_ref, stage_ref, sem_ref):
    i = pl.program_id(0)
    last = pl.num_programs(0) - 1
    slot = i % 2

    def seg_copy(s):
        return lambda loc, glob, size: pltpu.make_async_copy(
            buf_ref.at[pl.ds(glob, size), :], stage_ref.at[s, pl.ds(loc, size), :], sem_ref.at[s])

    @pl.when(i == 0)
    def _():
        stage_ref[...] = jnp.zeros(stage_ref.shape, BF16)
        _start_segments(pc_ref, off_ref, i, seg_copy(slot))

    _wait_segments(pc_ref, i, seg_copy(slot))

    @pl.when(i < last)
    def _():
        _start_segments(pc_ref, off_ref, i + 1, seg_copy(1 - slot))

    rt = routet_ref[...]
    col = lambda r: rt[:, r:r + 1]
    d1 = col(R_D1).astype(jnp.int32)
    d2 = col(R_D2).astype(jnp.int32)
    lanes = lax.broadcasted_iota(jnp.int32, (MOE_TT, MOE_RTILE), 1)
    y = stage_ref[slot]
    select = jnp.where(lanes == d1, col(R_W1), jnp.where(lanes == d2, col(R_W2), 0.0))
    f = _dot(select.astype(BF16), y)
    o_ref[...] = _layer_norm(DN_ALPHA * x_ref[...] + f, g_ref[...], b_ref[...])


def _combine(x, routet, pc, off, buf, g, b, *, layer):
    n, d = x.shape
    row = pl.BlockSpec((MOE_TT, d), lambda i, *_: (i, 0))
    return pl.pallas_call(
        _combine_kernel,
        grid_spec=pltpu.PrefetchScalarGridSpec(
            num_scalar_prefetch=2,
            grid=(n // MOE_TT,),
            in_specs=[row, pl.BlockSpec((MOE_TT, LANE), lambda i, *_: (i, 0)),
                      _layer_spec(g, layer), _layer_spec(b, layer), pl.BlockSpec(memory_space=pl.ANY)],
            out_specs=row,
            scratch_shapes=[pltpu.VMEM((2, MOE_RTILE, d), BF16), pltpu.SemaphoreType.DMA((2,))]),
        out_shape=jax.ShapeDtypeStruct((n, d), F32),
        compiler_params=_params(("arbitrary",)),
        name="combine",
    )(pc, off, x, routet, g, b, buf)


def _moe(x, wr_t, b_router, wgu, wd, g, b, *, layer, moe_layer):
    n, d = x.shape
    nt = n // MOE_TT
    route, routet, cnt = _router(x, wr_t, b_router, layer=moe_layer, tm=MOE_TT)

    cnt = cnt[:, :, 0].astype(jnp.int32)
    pc = (cnt + SEG_ALIGN - 1) // SEG_ALIGN * SEG_ALIGN
    total = jnp.sum(pc, axis=0)
    region = (total + MOE_RB - 1) // MOE_RB * MOE_RB
    base = jnp.cumsum(region) - region
    off = base[None, :] + jnp.cumsum(pc, axis=0) - pc
    rmax = (2 * n + nt * N_EXPERTS * (SEG_ALIGN - 1) + N_EXPERTS * (MOE_RB - 1) + MOE_RB - 1) // MOE_RB * MOE_RB
    blk_end = jnp.cumsum(region // MOE_RB)
    nused = blk_end[-1:].astype(jnp.int32)
    blk = jnp.minimum(jnp.arange(rmax // MOE_RB, dtype=jnp.int32), nused - 1)
    be = jnp.sum(blk[:, None] >= blk_end[None, :], axis=1).astype(jnp.int32)
    tail = jnp.concatenate([base + total, (region - total) // SEG_ALIGN]).astype(jnp.int32)
    pc, off = pc.reshape(-1), off.reshape(-1).astype(jnp.int32)

    buf = _dispatch(x, route, pc, off, tail, nused, rmax)
    y = _experts(buf, be, nused, wgu, wd, moe_layer=0)
    return _combine(x, routet, pc, off, y, g, b, layer=layer)


def _rotary_tables(seq):
    half = ROT_DIM // 2
    inv_freq = jnp.power(jnp.float32(ROPE_THETA), -jnp.arange(half, dtype=F32) / half)
    ang = jnp.arange(seq, dtype=jnp.int32).astype(F32)[:, None] * inv_freq[None, :]
    cos, sin = jnp.cos(ang), jnp.sin(ang)
    d = jnp.arange(LANE) % HEAD_DIM
    first, second = d < half, (d >= half) & (d < ROT_DIM)
    idx = d % half
    cos_t = jnp.where((first | second)[None, :], cos[:, idx], 1.0)
    sin_a = jnp.where(first[None, :], -sin[:, idx], 0.0)
    sin_b = jnp.where(second[None, :], sin[:, idx], 0.0)
    return cos_t, sin_a, sin_b


def kernel(x, mem, w_in, b_gate, attn_sinks, ssm_a_re, ssm_a_im, ssm_log_dt, ssm_b_re, ssm_b_im, ssm_c_re, ssm_c_im, ssm_d, ssm_w_glu, pool_w, pool_scale, w_br_attn, w_br_ssm, w_br_pool, w_o, ln1_g, ln1_b, xa_wq, xa_wkv, xa_wo, ln2_g, ln2_b, ffn_w_gu, ffn_w_down, moe_w_router, moe_b_router, moe_w_gu, moe_w_down, ln3_g, ln3_b):
    bsz, seq, d = x.shape
    n = bsz * seq
    nmem = mem.shape[1]
    cos_t, sin_a, sin_b = _rotary_tables(seq)
    mem2 = mem.reshape(bsz * nmem, d)
    bf = lambda w: w.astype(BF16)
    vec = lambda v: v.reshape(v.shape[0], 1, v.shape[1])

    w_in_b, pool_w_b, wglu_b = bf(w_in), bf(pool_w), bf(ssm_w_glu)
    merge_stacks = (w_in_b, vec(b_gate), bf(w_br_attn), bf(w_br_ssm), bf(w_br_pool), bf(w_o), vec(ln1_g), vec(ln1_b))
    xattn_stacks = (bf(xa_wq), bf(xa_wo), vec(ln2_g), vec(ln2_b))
    wkv_b = bf(xa_wkv)
    ffn_gu_b, ffn_down_b = bf(ffn_w_gu), bf(ffn_w_down)
    router_t = jnp.swapaxes(moe_w_router, 1, 2)
    router_b = moe_b_router.reshape(moe_b_router.shape[0], N_EXPERTS, 1)
    pool_scale3, ssm_d3, ln3_g3, ln3_b3 = vec(pool_scale), vec(ssm_d), vec(ln3_g), vec(ln3_b)

    for i in range(DEPTH):
        q, k, v, u, pm = _inproj(x, w_in_b, cos_t, sin_a, sin_b, pool_w_b, pool_scale3,
                                 layer=i, tt=min(TT_INPROJ, seq))
        attn = _swa(q, k, v, attn_sinks, layer=i, tq=min(TQ_SWA, seq))
        bw, cw, are, aim = _ssm_weights(ssm_a_re[i], ssm_a_im[i], ssm_log_dt[i], ssm_b_re[i], ssm_b_im[i],
                                        ssm_c_re[i], ssm_c_im[i], bsz)
        hs = _ssm(u, bw, cw, are, aim, ssm_d3, wglu_b, layer=i, tc=TC_SSM)
        x2 = _merge(x.reshape(n, d), attn.reshape(n, -1), hs.reshape(n, -1), pm.reshape(n, -1), merge_stacks,
                    layer=i, tm=TM_MERGE)

        kv = _matmul(mem2, wkv_b, layer=i, tm=512, out_dtype=BF16).reshape(bsz, nmem, 2 * d)
        x3 = _xattn(x2.reshape(bsz, seq, d), kv, xattn_stacks, layer=i, tm=min(TM_XATTN, seq)).reshape(n, d)

        j = i // 2
        if i % 2 == 0:
            x4, expert_gu_b, expert_down_b = _ffn(x3, ffn_gu_b, ffn_down_b, ln3_g3, ln3_b3, moe_w_gu, moe_w_down,
                                                  layer=i, ffn_layer=j, tm=TM_FFN, nchunk=FFN_CHUNKS)
        else:
            x4 = _moe(x3, router_t, router_b, expert_gu_b, expert_down_b, ln3_g3, ln3_b3, layer=i, moe_layer=j)
        x = x4.reshape(bsz, seq, d)
    return x
```

```python
import functools

import jax
import jax.numpy as jnp
from jax import lax
from jax.experimental import pallas as pl
from jax.experimental.pallas import tpu as pltpu

D_MODEL = 1024
DEPTH = 4
N_Q_HEADS = 8
N_KV_HEADS = 2
HEAD_DIM = 64
WINDOW = 128
ROT_DIM = HEAD_DIM // 4
ROPE_THETA = 500000.0
SSM_WIDTH = D_MODEL // 2
SSM_GROUP = 16
SSM_GROUPS = SSM_WIDTH // SSM_GROUP
SSM_STATE = 64
SSM_LANES = SSM_GROUPS * SSM_STATE
POOL_WINDOWS = (2, 4, 8, 16)
POOL_WIDTH = D_MODEL // 2
POOL_GROUP = POOL_WIDTH // len(POOL_WINDOWS)
POOL_HALO = 16
X_HEADS = 4
X_HEAD_DIM = D_MODEL // X_HEADS
D_FF = 2816
N_EXPERTS = 8
D_FF_EXPERT = 3584
Q_WIDTH = N_Q_HEADS * HEAD_DIM
KV_WIDTH = N_KV_HEADS * HEAD_DIM
QKV_WIDTH = Q_WIDTH + 2 * KV_WIDTH
BRANCH_IN_WIDTH = QKV_WIDTH + SSM_WIDTH + POOL_WIDTH
N_BRANCHES = 3
GATES_IN_INPROJ = 2
DN_ALPHA = (2.0 * DEPTH) ** 0.25
LN_EPS = 1e-5
NEG_INF = -1e30

TT_INPROJ = 1024
TQ_SWA = 1024
TC_SSM = 128
TM_MERGE = 1024
TM_XATTN = 1024
TM_FFN = 512
FFN_CHUNKS = 2

LANE = 128
MXU_TILE = 256
VMEM_LIMIT = 56 * 1024 * 1024

BF16 = jnp.bfloat16
F32 = jnp.float32


def _dot(a, b):
    return jnp.dot(a, b, preferred_element_type=F32)


def _dot_nt(a, b):
    return lax.dot_general(a, b, (((1,), (1,)), ((), ())), preferred_element_type=F32)


def _layer_norm(y, g, b):
    mu = jnp.mean(y, axis=-1, keepdims=True)
    d = y - mu
    var = jnp.mean(d * d, axis=-1, keepdims=True)
    return d * lax.rsqrt(var + LN_EPS) * g + b


def _branch_gates(xb, w_ref, bg_ref, gate_ref, first, count):
    for i in range(count):
        cols = slice((first + i) * D_MODEL, (first + i + 1) * D_MODEL)
        wcols = slice(BRANCH_IN_WIDTH + cols.start, BRANCH_IN_WIDTH + cols.stop)
        gate_ref[:, i * D_MODEL:(i + 1) * D_MODEL] = (
            jax.nn.sigmoid(_dot(xb, w_ref[:, wcols]) + bg_ref[:, cols]).astype(BF16))


def _const_spec(shape):
    nd = len(shape)
    return pl.BlockSpec(shape, lambda *_: (0,) * nd, pipeline_mode=pl.Buffered(1))


def _layer_spec(stack, layer, width=None, col=0):
    rest = stack.shape[1:]
    block = (None,) + rest[:-1] + (rest[-1] if width is None else width,)
    index = (layer,) + (0,) * (len(rest) - 1) + (col,)
    return pl.BlockSpec(block, lambda *_: index, pipeline_mode=pl.Buffered(1))


def _params(sem, limit=VMEM_LIMIT):
    return pltpu.CompilerParams(dimension_semantics=sem, vmem_limit_bytes=limit)


def _rotary(t, cos, sin_a, sin_b):
    return t * cos + pltpu.roll(t, LANE - ROT_DIM // 2, 1) * sin_a + pltpu.roll(t, ROT_DIM // 2, 1) * sin_b


def _inproj_kernel(x_ref, w_ref, bg_ref, cos_ref, sa_ref, sb_ref, pw_ref, ps_ref,
                   q_ref, k_ref, v_ref, u_ref, pm_ref, gate_ref, hist_ref, *, tt):
    ti = pl.program_id(1)
    xb = x_ref[...].astype(BF16)
    cos, sa, sb = cos_ref[...], sa_ref[...], sb_ref[...]

    qkv = _dot(xb, w_ref[:, 0:QKV_WIDTH])
    for c in range(Q_WIDTH // LANE):
        t = qkv[:, c * LANE:(c + 1) * LANE] * (HEAD_DIM ** -0.5)
        q_ref[:, c * LANE:(c + 1) * LANE] = _rotary(t, cos, sa, sb).astype(BF16)
    k_ref[...] = _rotary(qkv[:, Q_WIDTH:Q_WIDTH + KV_WIDTH], cos, sa, sb).astype(BF16)
    v_ref[...] = qkv[:, Q_WIDTH + KV_WIDTH:].astype(BF16)

    u_ref[...] = _dot(xb, w_ref[:, QKV_WIDTH:QKV_WIDTH + SSM_WIDTH])

    @pl.when(ti == 0)
    def _():
        hist_ref[0:POOL_HALO, :] = jnp.zeros((POOL_HALO, POOL_WIDTH), F32)

    up = _dot(xb, w_ref[:, QKV_WIDTH + SSM_WIDTH:BRANCH_IN_WIDTH])
    hist_ref[POOL_HALO:POOL_HALO + tt, :] = up
    pos1 = (ti * tt + 1 + lax.broadcasted_iota(jnp.int32, (tt, 1), 0)).astype(F32)
    for gi, w in enumerate(POOL_WINDOWS):
        lanes = slice(gi * POOL_GROUP, (gi + 1) * POOL_GROUP)
        acc = hist_ref[:, lanes]
        k = 1
        while k < w:
            acc = acc + pltpu.roll(acc, k, 0)
            k *= 2
        pooled = acc[POOL_HALO:, :] / jnp.minimum(pos1, float(w)) - up[:, lanes]
        mixed = _dot(pooled.astype(BF16), pw_ref[gi]) * ps_ref[:, lanes]
        pm_ref[:, lanes] = mixed.astype(BF16)
    hist_ref[0:POOL_HALO, :] = hist_ref[tt:tt + POOL_HALO, :]

    _branch_gates(xb, w_ref, bg_ref, gate_ref, 0, GATES_IN_INPROJ)


def _inproj(x, w_in, b_gate, cos, sa, sb, pool_w, pool_scale, *, layer, tt):
    bsz, seq, d = x.shape
    row = lambda w: pl.BlockSpec((None, tt, w), lambda b, t: (b, t, 0))
    tab = pl.BlockSpec((tt, LANE), lambda b, t: (t, 0))
    return pl.pallas_call(
        functools.partial(_inproj_kernel, tt=tt),
        grid=(bsz, seq // tt),
        in_specs=[row(d), _layer_spec(w_in, layer), _layer_spec(b_gate, layer), tab, tab, tab,
                  _layer_spec(pool_w, layer), _layer_spec(pool_scale, layer)],
        out_specs=[row(Q_WIDTH), row(KV_WIDTH), row(KV_WIDTH), row(SSM_WIDTH), row(POOL_WIDTH),
                   row(GATES_IN_INPROJ * D_MODEL)],
        out_shape=[jax.ShapeDtypeStruct((bsz, seq, Q_WIDTH), BF16),
                   jax.ShapeDtypeStruct((bsz, seq, KV_WIDTH), BF16),
                   jax.ShapeDtypeStruct((bsz, seq, KV_WIDTH), BF16),
                   jax.ShapeDtypeStruct((bsz, seq, SSM_WIDTH), F32),
                   jax.ShapeDtypeStruct((bsz, seq, POOL_WIDTH), BF16),
                   jax.ShapeDtypeStruct((bsz, seq, GATES_IN_INPROJ * D_MODEL), BF16)],
        scratch_shapes=[pltpu.VMEM((tt + POOL_HALO, POOL_WIDTH), F32)],
        compiler_params=_params(("arbitrary", "arbitrary")),
        name="inproj",
    )(x, w_in, b_gate, cos, sa, sb, pool_w, pool_scale)


def _half_lanes(a, b, lo):
    return jnp.where(lo, a, b)


def _swa_kernel(sink_ref, q_ref, k_ref, kp_ref, v_ref, vp_ref, x_ref, w_ref, bg_ref, o_ref, gate_ref, *, nsub):
    _branch_gates(x_ref[...].astype(BF16), w_ref, bg_ref, gate_ref, GATES_IN_INPROJ, N_BRANCHES - GATES_IN_INPROJ)

    ti = pl.program_id(1)
    blk = WINDOW
    lo = lax.broadcasted_iota(jnp.int32, (2 * blk, LANE), 1) < HEAD_DIM
    tri = (lax.broadcasted_iota(jnp.int32, (2 * blk, blk), 1)
           <= lax.broadcasted_iota(jnp.int32, (2 * blk, blk), 0) % blk)
    top = lax.broadcasted_iota(jnp.int32, (2 * blk, 1), 0) < blk
    zero = jnp.zeros((2 * blk, LANE), F32)
    ones_sel = ((lax.broadcasted_iota(jnp.int32, (4 * blk, LANE), 0) < 2 * blk)
                == (lax.broadcasted_iota(jnp.int32, (4 * blk, LANE), 1) < HEAD_DIM)).astype(BF16)

    for j in range(nsub):
        if j == 0:
            kc = jnp.concatenate([kp_ref[...], k_ref[0:blk, :]], axis=0)
            vc = jnp.concatenate([vp_ref[...], v_ref[0:blk, :]], axis=0)
        else:
            kc = k_ref[(j - 1) * blk:(j + 1) * blk, :]
            vc = v_ref[(j - 1) * blk:(j + 1) * blk, :]
        kc = kc.astype(F32)
        vc = vc.astype(F32)
        kr = pltpu.roll(kc, HEAD_DIM, 1)
        vr = pltpu.roll(vc, HEAD_DIM, 1)
        kmat = [[_half_lanes(kc, zero, lo), _half_lanes(zero, kr, lo)],
                [_half_lanes(kr, zero, lo), _half_lanes(zero, kc, lo)]]
        vmat = [[_half_lanes(vc, zero, lo), _half_lanes(zero, vr, lo)],
                [_half_lanes(vr, zero, lo), _half_lanes(zero, vc, lo)]]
        rows = pl.ds(j * blk, blk)
        for kh in range(N_KV_HEADS):
            base = kh * 2 * LANE
            qq = jnp.concatenate([q_ref[rows, base:base + LANE], q_ref[rows, base + LANE:base + 2 * LANE]], axis=0)
            probs, sink_term = [], []
            for par in range(2):
                s = _dot_nt(qq, kmat[kh][par].astype(BF16))
                s_prev = s[:, :blk] if j > 0 else jnp.where(ti > 0, s[:, :blk], NEG_INF)
                s = jnp.where(tri, s[:, blk:], s_prev)
                h_top = kh * 4 + par
                sink = jnp.where(top, sink_ref[h_top], sink_ref[h_top + 2])
                m = jnp.maximum(jnp.max(s, axis=-1, keepdims=True), sink)
                p = jnp.exp(s - m)
                sink_term.append(jnp.exp(sink - m))
                probs.append(jnp.concatenate([jnp.where(tri, 0.0, p), jnp.where(tri, p, 0.0)], axis=1).astype(BF16))
            pcat = jnp.concatenate(probs, axis=1)
            vcat = jnp.concatenate([vmat[kh][0], vmat[kh][1]], axis=0).astype(BF16)
            ov = _dot(pcat, jnp.concatenate([vcat, ones_sel], axis=1))
            den = ov[:, LANE:] + jnp.where(lo, sink_term[0], sink_term[1])
            o = ov[:, :LANE] / den
            o_ref[rows, base:base + LANE] = o[0:blk].astype(BF16)
            o_ref[rows, base + LANE:base + 2 * LANE] = o[blk:2 * blk].astype(BF16)


def _swa(q, k, v, sinks, x, w_in, b_gate, *, layer, tq):
    bsz, seq, _ = q.shape
    nsub = tq // WINDOW
    cur = lambda w: pl.BlockSpec((None, tq, w), lambda b, t: (b, t, 0))
    prev = pl.BlockSpec((None, WINDOW, KV_WIDTH), lambda b, t: (b, jnp.maximum(t * nsub - 1, 0), 0))
    return pl.pallas_call(
        functools.partial(_swa_kernel, nsub=nsub),
        grid=(bsz, seq // tq),
        in_specs=[pl.BlockSpec(memory_space=pltpu.SMEM), cur(Q_WIDTH), cur(KV_WIDTH), prev, cur(KV_WIDTH), prev,
                  cur(D_MODEL), _layer_spec(w_in, layer), _layer_spec(b_gate, layer)],
        out_specs=[cur(Q_WIDTH), cur((N_BRANCHES - GATES_IN_INPROJ) * D_MODEL)],
        out_shape=[jax.ShapeDtypeStruct((bsz, seq, Q_WIDTH), BF16),
                   jax.ShapeDtypeStruct((bsz, seq, (N_BRANCHES - GATES_IN_INPROJ) * D_MODEL), BF16)],
        compiler_params=_params(("parallel", "parallel")),
        name="swa",
    )(sinks[layer], q, k, k, v, v, x, w_in, b_gate)


SSM_CHUNKS = 4
SSM_CHUNK_LANES = SSM_LANES // SSM_CHUNKS
SSM_CHUNK_IN = SSM_WIDTH // SSM_CHUNKS
SCAN_LANES = 1024


def _ssm_kernel(u_ref, bw_ref, cw_ref, are_ref, aim_ref, d_ref, wglu_ref, o_ref, xs_ref, st_ref, h_ref,
                *, tc, bsz):
    @pl.when(pl.program_id(0) == 0)
    def _():
        h_ref[...] = jnp.zeros(h_ref.shape, F32)

    for b in range(bsz):
        for c in range(SSM_CHUNKS):
            xs_ref[c, pl.ds(b, tc, stride=bsz), :] = u_ref[b, :, c * SSM_CHUNK_IN:(c + 1) * SSM_CHUNK_IN]

    for part in range(2):
        for c in range(SSM_CHUNKS):
            col = part * SSM_LANES + c * SSM_CHUNK_LANES
            st_ref[:, col:col + SSM_CHUNK_LANES] = _dot(xs_ref[c].astype(BF16), bw_ref[part * SSM_CHUNKS + c])

    for jb in range(SSM_LANES // SCAN_LANES):
        re = pl.ds(jb * SCAN_LANES, SCAN_LANES)
        im = pl.ds(SSM_LANES + jb * SCAN_LANES, SCAN_LANES)
        ar = are_ref[:, re]
        ai = aim_ref[:, re]

        def step(t, carry):
            hr, hi = carry
            r = pl.ds(pl.multiple_of(t * bsz, bsz), bsz)
            nr = ar * hr - ai * hi + st_ref[r, re]
            ni = ar * hi + ai * hr + st_ref[r, im]
            st_ref[r, re] = nr
            st_ref[r, im] = ni
            return nr, ni

        hr, hi = lax.fori_loop(0, tc, step, (h_ref[:, re], h_ref[:, im]), unroll=8)
        h_ref[:, re] = hr
        h_ref[:, im] = hi

    ys = []
    for c in range(SSM_CHUNKS):
        sre = st_ref[:, c * SSM_CHUNK_LANES:(c + 1) * SSM_CHUNK_LANES].astype(BF16)
        sim = st_ref[:, SSM_LANES + c * SSM_CHUNK_LANES:SSM_LANES + (c + 1) * SSM_CHUNK_LANES].astype(BF16)
        cols = slice(c * SSM_CHUNK_IN, (c + 1) * SSM_CHUNK_IN)
        y = _dot(sre, cw_ref[c]) + _dot(sim, cw_ref[SSM_CHUNKS + c])
        ys.append(y + d_ref[:, cols] * xs_ref[c])

    hh = jax.nn.gelu(jnp.concatenate(ys, axis=1)).astype(BF16)
    z = _dot(hh, wglu_ref[...])
    res = z[:, :SSM_WIDTH] * jax.nn.sigmoid(z[:, SSM_WIDTH:])
    for c in range(SSM_CHUNKS):
        xs_ref[c] = res[:, c * SSM_CHUNK_IN:(c + 1) * SSM_CHUNK_IN]
    for b in range(bsz):
        for c in range(SSM_CHUNKS):
            o_ref[b, :, c * SSM_CHUNK_IN:(c + 1) * SSM_CHUNK_IN] = (
                xs_ref[c, pl.ds(b, tc, stride=bsz), :].astype(BF16))


def _ssm(u, bw, cw, are, aim, d_skip, wglu, *, layer, tc):
    bsz, seq, _ = u.shape
    blk = pl.BlockSpec((bsz, tc, SSM_WIDTH), lambda t: (0, t, 0))
    return pl.pallas_call(
        functools.partial(_ssm_kernel, tc=tc, bsz=bsz),
        grid=(seq // tc,),
        in_specs=[blk, _const_spec(bw.shape), _const_spec(cw.shape), _const_spec(are.shape),
                  _const_spec(aim.shape), _layer_spec(d_skip, layer), _layer_spec(wglu, layer)],
        out_specs=blk,
        out_shape=jax.ShapeDtypeStruct((bsz, seq, SSM_WIDTH), BF16),
        scratch_shapes=[pltpu.VMEM((SSM_CHUNKS, tc * bsz, SSM_CHUNK_IN), F32),
                        pltpu.VMEM((tc * bsz, 2 * SSM_LANES), F32),
                        pltpu.VMEM((bsz, 2 * SSM_LANES), F32)],
        compiler_params=_params(("arbitrary",)),
        name="ssm",
    )(u, bw, cw, are, aim, d_skip, wglu)


def _ssm_weights(a_re, a_im, log_dt, b_re, b_im, c_re, c_im, bsz):
    dt = jnp.exp(log_dt)[:, None]
    decay = jnp.exp(dt * a_re)
    abar_re, abar_im = decay * jnp.cos(dt * a_im), decay * jnp.sin(dt * a_im)
    inv_abs2 = 1.0 / (a_re * a_re + a_im * a_im)
    num_re, num_im = abar_re - 1.0, abar_im
    f_re = (num_re * a_re + num_im * a_im) * inv_abs2
    f_im = (num_im * a_re - num_re * a_im) * inv_abs2
    bbar_re = f_re[..., None] * b_re - f_im[..., None] * b_im
    bbar_im = f_re[..., None] * b_im + f_im[..., None] * b_re
    gpc = SSM_GROUPS // SSM_CHUNKS
    eye = jnp.eye(gpc, dtype=F32)

    def in_chunks(bb):
        t = bb.reshape(SSM_CHUNKS, gpc, SSM_STATE, SSM_GROUP)
        return jnp.einsum('cgnp,gh->cgphn', t, eye).reshape(SSM_CHUNKS, gpc * SSM_GROUP, gpc * SSM_STATE)

    def out_chunks(cc):
        t = cc.reshape(SSM_CHUNKS, gpc, SSM_GROUP, SSM_STATE)
        return jnp.einsum('cgpn,gh->cgnhp', t, eye).reshape(SSM_CHUNKS, gpc * SSM_STATE, gpc * SSM_GROUP)

    bw = jnp.concatenate([in_chunks(bbar_re), in_chunks(bbar_im)], axis=0).astype(BF16)
    cw = jnp.concatenate([out_chunks(c_re), out_chunks(-c_im)], axis=0).astype(BF16)
    are = jnp.broadcast_to(abar_re.reshape(1, SSM_LANES), (bsz, SSM_LANES))
    aim = jnp.broadcast_to(abar_im.reshape(1, SSM_LANES), (bsz, SSM_LANES))
    return bw, cw, are, aim


def _merge_kernel(x_ref, a_ref, s_ref, p_ref, ga_ref, gb_ref, wa_ref, ws_ref, wp_ref, wo_ref, g_ref, b_ref, o_ref):
    merged = None
    for i, (br_ref, w_ref) in enumerate(((a_ref, wa_ref), (s_ref, ws_ref), (p_ref, wp_ref))):
        gate_ref, gi = (ga_ref, i) if i < GATES_IN_INPROJ else (gb_ref, i - GATES_IN_INPROJ)
        gate = gate_ref[:, gi * D_MODEL:(gi + 1) * D_MODEL].astype(F32)
        term = gate * _dot(br_ref[...], w_ref[...])
        merged = term if merged is None else merged + term
    h = _dot(merged.astype(BF16), wo_ref[...])
    o_ref[...] = _layer_norm(DN_ALPHA * x_ref[...] + h, g_ref[...], b_ref[...])


def _merge(x, a, s, p, gates_a, gates_b, stacks, *, layer, tm):
    n, d = x.shape
    row = lambda w: pl.BlockSpec((tm, w), lambda i: (i, 0))
    return pl.pallas_call(
        _merge_kernel,
        grid=(n // tm,),
        in_specs=([row(d), row(Q_WIDTH), row(SSM_WIDTH), row(POOL_WIDTH), row(gates_a.shape[1]), row(gates_b.shape[1])]
                  + [_layer_spec(c, layer) for c in stacks]),
        out_specs=row(d),
        out_shape=jax.ShapeDtypeStruct((n, d), F32),
        compiler_params=_params(("parallel",)),
        name="merge",
    )(x, a, s, p, gates_a, gates_b, *stacks)


def _matmul_kernel(a_ref, w_ref, o_ref):
    o_ref[...] = _dot(a_ref[...].astype(BF16), w_ref[...]).astype(o_ref.dtype)


def _matmul(a, w, *, layer, tm, out_dtype):
    m, k = a.shape
    n = w.shape[-1]
    return pl.pallas_call(
        _matmul_kernel,
        grid=(m // tm,),
        in_specs=[pl.BlockSpec((tm, k), lambda i: (i, 0)), _layer_spec(w, layer)],
        out_specs=pl.BlockSpec((tm, n), lambda i: (i, 0)),
        out_shape=jax.ShapeDtypeStruct((m, n), out_dtype),
        compiler_params=_params(("parallel",)),
        name="kvproj",
    )(a, w)


def _xattn_kernel(x_ref, k_ref, v_ref, wq_ref, wo_ref, g_ref, b_ref, o_ref):
    x = x_ref[...]
    q = (_dot(x.astype(BF16), wq_ref[...]) * (X_HEAD_DIM ** -0.5)).astype(BF16)
    outs = []
    for h in range(X_HEADS):
        cols = slice(h * X_HEAD_DIM, (h + 1) * X_HEAD_DIM)
        s = _dot_nt(q[:, cols], k_ref[:, cols])
        p = jnp.exp(s - jnp.max(s, axis=-1, keepdims=True))
        inv = 1.0 / jnp.sum(p, axis=-1, keepdims=True)
        outs.append((_dot(p.astype(BF16), v_ref[:, cols]) * inv).astype(BF16))
    c = _dot(jnp.concatenate(outs, axis=1), wo_ref[...])
    o_ref[...] = _layer_norm(DN_ALPHA * x + c, g_ref[...], b_ref[...])


def _xattn(x, kv, stacks, *, layer, tm):
    bsz, seq, d = x.shape
    nmem = kv.shape[1]
    row = pl.BlockSpec((None, tm, d), lambda bi, t: (bi, t, 0))
    kspec = pl.BlockSpec((None, nmem, d), lambda bi, t: (bi, 0, 0))
    vspec = pl.BlockSpec((None, nmem, d), lambda bi, t: (bi, 0, 1))
    return pl.pallas_call(
        _xattn_kernel,
        grid=(bsz, seq // tm),
        in_specs=[row, kspec, vspec] + [_layer_spec(c, layer) for c in stacks],
        out_specs=row,
        out_shape=jax.ShapeDtypeStruct((bsz, seq, d), F32),
        compiler_params=_params(("parallel", "parallel")),
        name="xattn",
    )(x, kv, kv, *stacks)


def _ff_chunks(width, nchunk):
    tiles = width // MXU_TILE
    assert tiles * MXU_TILE == width
    bounds = [MXU_TILE * ((tiles * c) // nchunk) for c in range(nchunk + 1)]
    return [slice(lo, hi) for lo, hi in zip(bounds[:-1], bounds[1:])]


def _swiglu(xb, wg_ref, wu_ref, wd_ref, nchunk):
    acc = None
    for cols in _ff_chunks(wg_ref.shape[1], nchunk):
        hid = jax.nn.silu(_dot(xb, wg_ref[:, cols])) * _dot(xb, wu_ref[:, cols])
        part = _dot(hid.astype(BF16), wd_ref[cols, :])
        acc = part if acc is None else acc + part
    return acc


def _ffn_kernel(x_ref, wg_ref, wu_ref, wd_ref, g_ref, b_ref, egu_ref, edn_ref, o_ref, egu_out_ref, edn_out_ref,
                *, nchunk):
    x = x_ref[...]
    acc = _swiglu(x.astype(BF16), wg_ref, wu_ref, wd_ref, nchunk)
    o_ref[...] = _layer_norm(DN_ALPHA * x + acc, g_ref[...], b_ref[...])
    egu_out_ref[...] = egu_ref[...].astype(BF16)
    edn_out_ref[...] = edn_ref[...].astype(BF16)


def _ffn(x, wgu, wd, g, b, expert_gu, expert_down, *, layer, ffn_layer, tm, nchunk):
    n, d = x.shape
    ff = wd.shape[1]
    steps = n // tm
    _, ne, gu_rows, gu_cols = expert_gu.shape
    _, _, dn_rows, dn_cols = expert_down.shape
    per_expert = steps // ne
    assert per_expert * ne == steps

    def slices(rows, cols):
        blk = (None, None, rows // per_expert, cols)
        return (pl.BlockSpec(blk, lambda i: (ffn_layer, i // per_expert, i % per_expert, 0)),
                pl.BlockSpec(blk, lambda i: (0, i // per_expert, i % per_expert, 0)))

    gu_in, gu_out = slices(gu_rows, gu_cols)
    dn_in, dn_out = slices(dn_rows, dn_cols)
    row = pl.BlockSpec((tm, d), lambda i: (i, 0))
    return pl.pallas_call(
        functools.partial(_ffn_kernel, nchunk=nchunk),
        grid=(steps,),
        in_specs=[row, _layer_spec(wgu, ffn_layer, width=ff, col=0), _layer_spec(wgu, ffn_layer, width=ff, col=1),
                  _layer_spec(wd, ffn_layer), _layer_spec(g, layer), _layer_spec(b, layer), gu_in, dn_in],
        out_specs=[row, gu_out, dn_out],
        out_shape=[jax.ShapeDtypeStruct((n, d), F32),
                   jax.ShapeDtypeStruct((1,) + expert_gu.shape[1:], BF16),
                   jax.ShapeDtypeStruct((1,) + expert_down.shape[1:], BF16)],
        compiler_params=_params(("parallel",)),
        name="ffn",
    )(x, wgu, wgu, wd, g, b, expert_gu, expert_down)


MOE_TT = 512
SEG_ALIGN = 16
SEG_BIG = 128
MOE_RB = 512
MOE_RTILE = 2 * MOE_TT + N_EXPERTS * SEG_ALIGN
MOE_FF_CHUNKS = 2
R_D1, R_D2, R_W1, R_W2 = range(4)


def _router_kernel(x_ref, wr_ref, br_ref, before_ref, route_ref, routet_ref, cnt_ref, *, tm):
    x, wr = x_ref[...], wr_ref[...]
    x_hi, wr_hi = x.astype(BF16), wr.astype(BF16)
    x_lo, wr_lo = (x - x_hi.astype(F32)).astype(BF16), (wr - wr_hi.astype(F32)).astype(BF16)
    logits = _dot_nt(wr_hi, x_hi) + _dot_nt(wr_hi, x_lo) + _dot_nt(wr_lo, x_hi) + br_ref[...]
    eid = lax.broadcasted_iota(jnp.int32, logits.shape, 0)
    v1 = jnp.max(logits, axis=0, keepdims=True)
    e1 = jnp.min(jnp.where(logits == v1, eid, N_EXPERTS), axis=0, keepdims=True)
    rest = jnp.where(eid == e1, -jnp.inf, logits)
    v2 = jnp.max(rest, axis=0, keepdims=True)
    e2 = jnp.min(jnp.where(rest == v2, eid, N_EXPERTS), axis=0, keepdims=True)
    t = jnp.exp(v2 - v1)
    w1 = 1.0 / (1.0 + t)
    w2 = t / (1.0 + t)
    sel1, sel2 = eid == e1, eid == e2
    member = (sel1 | sel2).astype(BF16)
    rank = _dot(member, before_ref[...])
    cnt = jnp.sum(member.astype(F32), axis=1, keepdims=True)
    cnt_ref[...] = jnp.broadcast_to(cnt, (N_EXPERTS, LANE))
    padded_cnt = jnp.floor((cnt + (SEG_ALIGN - 1)) * (1.0 / SEG_ALIGN)) * SEG_ALIGN
    seg = jnp.broadcast_to(padded_cnt, logits.shape)
    d1 = jnp.sum(jnp.where(eid < e1, seg, 0.0) + jnp.where(sel1, rank, 0.0), axis=0, keepdims=True)
    d2 = jnp.sum(jnp.where(eid < e2, seg, 0.0) + jnp.where(sel2, rank, 0.0), axis=0, keepdims=True)
    route = jnp.zeros(logits.shape, F32)
    for r, val in ((R_D1, d1), (R_D2, d2), (R_W1, w1), (R_W2, w2)):
        route = jnp.where(eid == r, val, route)
    route_ref[...] = route
    padded = jnp.concatenate([route, jnp.zeros((LANE - N_EXPERTS, tm), F32)], axis=0)
    routet_ref[...] = padded.T


def _router(x, wr_t, br, *, layer, tm):
    n, d = x.shape
    nt = n // tm
    before = (jnp.arange(tm)[:, None] < jnp.arange(tm)[None, :]).astype(BF16)
    return pl.pallas_call(
        functools.partial(_router_kernel, tm=tm),
        grid=(nt,),
        in_specs=[pl.BlockSpec((tm, d), lambda i: (i, 0)), _layer_spec(wr_t, layer), _layer_spec(br, layer),
                  _const_spec(before.shape)],
        out_specs=[pl.BlockSpec((N_EXPERTS, tm), lambda i: (0, i)),
                   pl.BlockSpec((tm, LANE), lambda i: (i, 0)),
                   pl.BlockSpec((None, N_EXPERTS, LANE), lambda i: (i, 0, 0))],
        out_shape=[jax.ShapeDtypeStruct((N_EXPERTS, n), F32),
                   jax.ShapeDtypeStruct((n, LANE), F32),
                   jax.ShapeDtypeStruct((nt, N_EXPERTS, LANE), F32)],
        compiler_params=_params(("parallel",)),
        name="router",
    )(x, wr_t, br, before)


def _start_segments(pc_ref, off_ref, tile, make_copy):
    loc = jnp.int32(0)
    for e in range(N_EXPERTS):
        rows = pc_ref[tile * N_EXPERTS + e]
        glob = off_ref[tile * N_EXPERTS + e]
        nbig = rows // SEG_BIG
        for size, first, count in ((SEG_BIG, 0, nbig), (SEG_ALIGN, nbig * SEG_BIG, (rows % SEG_BIG) // SEG_ALIGN)):
            def start(p, carry, loc=loc + first, glob=glob + first, size=size):
                make_copy(pl.multiple_of(loc + p * size, SEG_ALIGN),
                          pl.multiple_of(glob + p * size, SEG_ALIGN), size).start()
                return carry

            lax.fori_loop(0, count, start, 0)
        loc = loc + rows


def _wait_segments(pc_ref, tile, make_copy):
    nbig = nsmall = jnp.int32(0)
    for e in range(N_EXPERTS):
        rows = pc_ref[tile * N_EXPERTS + e]
        nbig = nbig + rows // SEG_BIG
        nsmall = nsmall + (rows % SEG_BIG) // SEG_ALIGN
    for size, count in ((SEG_BIG, nbig), (SEG_ALIGN, nsmall)):
        lax.fori_loop(0, count, lambda p, carry, size=size: (make_copy(0, 0, size).wait(), carry)[1], 0)


def _dispatch_kernel(pc_ref, off_ref, tail_ref, nused_ref, x_ref, route_ref, buf_ref, stage_ref, zero_ref, sem_ref,
                     *, nblocks):
    i = pl.program_id(0)
    last = pl.num_programs(0) - 1
    slot = i % 2
    d1 = route_ref[R_D1:R_D1 + 1, :].astype(jnp.int32)
    d2 = route_ref[R_D2:R_D2 + 1, :].astype(jnp.int32)
    rows = lax.broadcasted_iota(jnp.int32, (MOE_RTILE, MOE_TT), 0)
    onehot = ((rows == d1) | (rows == d2)).astype(BF16)
    stage_ref[slot] = _dot(onehot, x_ref[...].astype(BF16)).astype(BF16)

    def seg_copy(s):
        return lambda loc, glob, size: pltpu.make_async_copy(
            stage_ref.at[s, pl.ds(loc, size), :], buf_ref.at[pl.ds(glob, size), :], sem_ref.at[s])

    _start_segments(pc_ref, off_ref, i, seg_copy(slot))

    @pl.when(i > 0)
    def _():
        _wait_segments(pc_ref, i - 1, seg_copy(1 - slot))

    @pl.when(i == last)
    def _():
        _wait_segments(pc_ref, i, seg_copy(slot))
        zero_ref[...] = jnp.zeros(zero_ref.shape, BF16)

        def piece(glob):
            return pltpu.make_async_copy(zero_ref.at[pl.ds(0, SEG_ALIGN), :],
                                         buf_ref.at[pl.ds(glob, SEG_ALIGN), :], sem_ref.at[2])

        def block(row):
            return pltpu.make_async_copy(zero_ref, buf_ref.at[pl.ds(row, MOE_RB), :], sem_ref.at[2])

        npieces = jnp.int32(0)
        for e in range(N_EXPERTS):
            start, cnt = tail_ref[e], tail_ref[N_EXPERTS + e]
            lax.fori_loop(0, cnt, lambda p, c, start=start: (
                piece(pl.multiple_of(start + p * SEG_ALIGN, SEG_ALIGN)).start(), c)[1], 0)
            npieces = npieces + cnt
        lax.fori_loop(0, npieces, lambda p, c: (piece(0).wait(), c)[1], 0)
        lax.fori_loop(nused_ref[0], nblocks,
                      lambda b, c: (block(pl.multiple_of(b * MOE_RB, MOE_RB)).start(), c)[1], 0)
        lax.fori_loop(nused_ref[0], nblocks, lambda b, c: (block(0).wait(), c)[1], 0)


def _dispatch(x, route, pc, off, tail, nused, rmax):
    n, d = x.shape
    return pl.pallas_call(
        functools.partial(_dispatch_kernel, nblocks=rmax // MOE_RB),
        grid_spec=pltpu.PrefetchScalarGridSpec(
            num_scalar_prefetch=4,
            grid=(n // MOE_TT,),
            in_specs=[pl.BlockSpec((MOE_TT, d), lambda i, *_: (i, 0)),
                      pl.BlockSpec((N_EXPERTS, MOE_TT), lambda i, *_: (0, i))],
            out_specs=pl.BlockSpec(memory_space=pl.ANY),
            scratch_shapes=[pltpu.VMEM((2, MOE_RTILE, d), BF16), pltpu.VMEM((MOE_RB, d), BF16),
                            pltpu.SemaphoreType.DMA((3,))]),
        out_shape=jax.ShapeDtypeStruct((rmax, d), BF16),
        compiler_params=_params(("arbitrary",)),
        name="dispatch",
    )(pc, off, tail, nused, x, route)


def _experts_kernel(be_ref, nused_ref, x_ref, wg_ref, wu_ref, wd_ref, o_ref):
    @pl.when(pl.program_id(0) < nused_ref[0])
    def _():
        o_ref[...] = _swiglu(x_ref[...], wg_ref, wu_ref, wd_ref, MOE_FF_CHUNKS).astype(BF16)

    @pl.when(pl.program_id(0) >= nused_ref[0])
    def _():
        o_ref[...] = jnp.zeros(o_ref.shape, BF16)


def _experts(buf, be, nused, wgu, wd, *, moe_layer):
    rmax, d = buf.shape
    ff = wd.shape[2]
    blk = pl.BlockSpec((MOE_RB, d), lambda i, be, nu: (jnp.maximum(jnp.minimum(i, nu[0] - 1), 0), 0))
    out_blk = pl.BlockSpec((MOE_RB, d), lambda i, be, nu: (i, 0))
    once = pl.Buffered(1)
    return pl.pallas_call(
        _experts_kernel,
        grid_spec=pltpu.PrefetchScalarGridSpec(
            num_scalar_prefetch=2,
            grid=(rmax // MOE_RB,),
            in_specs=[blk,
                      pl.BlockSpec((None, None, d, ff), lambda i, be, nu: (moe_layer, be[i], 0, 0), pipeline_mode=once),
                      pl.BlockSpec((None, None, d, ff), lambda i, be, nu: (moe_layer, be[i], 0, 1), pipeline_mode=once),
                      pl.BlockSpec((None, None, ff, d), lambda i, be, nu: (moe_layer, be[i], 0, 0), pipeline_mode=once)],
            out_specs=out_blk),
        out_shape=jax.ShapeDtypeStruct((rmax, d), BF16),
        compiler_params=_params(("arbitrary",)),
        name="experts",
    )(be, nused, buf, wgu, wgu, wd)


def _combine_kernel(pc_ref, off_ref, x_ref, routet_ref, g_ref, b_ref, buf_ref, o_ref, stage_ref, sem_ref):
    i = pl.program_id(0)
    last = pl.num_programs(0) - 1
    slot = i % 2

    def seg_copy(s):
        return lambda loc, glob, size: pltpu.make_async_copy(
            buf_ref.at[pl.ds(glob, size), :], stage_ref.at[s, pl.ds(loc, size), :], sem_ref.at[s])

    @pl.when(i == 0)
    def _():
        stage_ref[...] = jnp.zeros(stage_ref.shape, BF16)
        _start_segments(pc_ref, off_ref, i, seg_copy(slot))

    _wait_segments(pc_ref, i, seg_copy(slot))

    @pl.when(i < last)
    def _():
        _start_segments(pc_ref, off_ref, i + 1, seg_copy(1 - slot))

    rt = routet_ref[...]
    col = lambda r: rt[:, r:r + 1]
    d1 = col(R_D1).astype(jnp.int32)
    d2 = col(R_D2).astype(jnp.int32)
    lanes = lax.broadcasted_iota(jnp.int32, (MOE_TT, MOE_RTILE), 1)
    y = stage_ref[slot]
    select = jnp.where(lanes == d1, col(R_W1), jnp.where(lanes == d2, col(R_W2), 0.0))
    f = _dot(select.astype(BF16), y)
    o_ref[...] = _layer_norm(DN_ALPHA * x_ref[...] + f, g_ref[...], b_ref[...])


def _combine(x, routet, pc, off, buf, g, b, *, layer):
    n, d = x.shape
    row = pl.BlockSpec((MOE_TT, d), lambda i, *_: (i, 0))
    return pl.pallas_call(
        _combine_kernel,
        grid_spec=pltpu.PrefetchScalarGridSpec(
            num_scalar_prefetch=2,
            grid=(n // MOE_TT,),
            in_specs=[row, pl.BlockSpec((MOE_TT, LANE), lambda i, *_: (i, 0)),
                      _layer_spec(g, layer), _layer_spec(b, layer), pl.BlockSpec(memory_space=pl.ANY)],
            out_specs=row,
            scratch_shapes=[pltpu.VMEM((2, MOE_RTILE, d), BF16), pltpu.SemaphoreType.DMA((2,))]),
        out_shape=jax.ShapeDtypeStruct((n, d), F32),
        compiler_params=_params(("arbitrary",)),
        name="combine",
    )(pc, off, x, routet, g, b, buf)


def _moe(x, wr_t, b_router, wgu, wd, g, b, *, layer, moe_layer):
    n, d = x.shape
    nt = n // MOE_TT
    route, routet, cnt = _router(x, wr_t, b_router, layer=moe_layer, tm=MOE_TT)

    cnt = cnt[:, :, 0].astype(jnp.int32)
    pc = (cnt + SEG_ALIGN - 1) // SEG_ALIGN * SEG_ALIGN
    total = jnp.sum(pc, axis=0)
    region = (total + MOE_RB - 1) // MOE_RB * MOE_RB
    base = jnp.cumsum(region) - region
    off = base[None, :] + jnp.cumsum(pc, axis=0) - pc
    rmax = (2 * n + nt * N_EXPERTS * (SEG_ALIGN - 1) + N_EXPERTS * (MOE_RB - 1) + MOE_RB - 1) // MOE_RB * MOE_RB
    blk_end = jnp.cumsum(region // MOE_RB)
    nused = blk_end[-1:].astype(jnp.int32)
    blk = jnp.minimum(jnp.arange(rmax // MOE_RB, dtype=jnp.int32), nused - 1)
    be = jnp.sum(blk[:, None] >= blk_end[None, :], axis=1).astype(jnp.int32)
    tail = jnp.concatenate([base + total, (region - total) // SEG_ALIGN]).astype(jnp.int32)
    pc, off = pc.reshape(-1), off.reshape(-1).astype(jnp.int32)

    buf = _dispatch(x, route, pc, off, tail, nused, rmax)
    y = _experts(buf, be, nused, wgu, wd, moe_layer=0)
    return _combine(x, routet, pc, off, y, g, b, layer=layer)


def _rotary_tables(seq):
    half = ROT_DIM // 2
    inv_freq = jnp.power(jnp.float32(ROPE_THETA), -jnp.arange(half, dtype=F32) / half)
    ang = jnp.arange(seq, dtype=jnp.int32).astype(F32)[:, None] * inv_freq[None, :]
    cos, sin = jnp.cos(ang), jnp.sin(ang)
    d = jnp.arange(LANE) % HEAD_DIM
    first, second = d < half, (d >= half) & (d < ROT_DIM)
    idx = d % half
    cos_t = jnp.where((first | second)[None, :], cos[:, idx], 1.0)
    sin_a = jnp.where(first[None, :], -sin[:, idx], 0.0)
    sin_b = jnp.where(second[None, :], sin[:, idx], 0.0)
    return cos_t, sin_a, sin_b


def kernel(x, mem, w_in, b_gate, attn_sinks, ssm_a_re, ssm_a_im, ssm_log_dt, ssm_b_re, ssm_b_im, ssm_c_re, ssm_c_im, ssm_d, ssm_w_glu, pool_w, pool_scale, w_br_attn, w_br_ssm, w_br_pool, w_o, ln1_g, ln1_b, xa_wq, xa_wkv, xa_wo, ln2_g, ln2_b, ffn_w_gu, ffn_w_down, moe_w_router, moe_b_router, moe_w_gu, moe_w_down, ln3_g, ln3_b):
    bsz, seq, d = x.shape
    n = bsz * seq
    nmem = mem.shape[1]
    cos_t, sin_a, sin_b = _rotary_tables(seq)
    mem2 = mem.reshape(bsz * nmem, d)
    bf = lambda w: w.astype(BF16)
    vec = lambda v: v.reshape(v.shape[0], 1, v.shape[1])

    w_in_b, pool_w_b, wglu_b = bf(w_in), bf(pool_w), bf(ssm_w_glu)
    b_gate3 = vec(b_gate)
    merge_stacks = (bf(w_br_attn), bf(w_br_ssm), bf(w_br_pool), bf(w_o), vec(ln1_g), vec(ln1_b))
    xattn_stacks = (bf(xa_wq), bf(xa_wo), vec(ln2_g), vec(ln2_b))
    wkv_b = bf(xa_wkv)
    ffn_gu_b, ffn_down_b = bf(ffn_w_gu), bf(ffn_w_down)
    router_t = jnp.swapaxes(moe_w_router, 1, 2)
    router_b = moe_b_router.reshape(moe_b_router.shape[0], N_EXPERTS, 1)
    pool_scale3, ssm_d3, ln3_g3, ln3_b3 = vec(pool_scale), vec(ssm_d), vec(ln3_g), vec(ln3_b)

    for i in range(DEPTH):
        q, k, v, u, pm, gates_a = _inproj(x, w_in_b, b_gate3, cos_t, sin_a, sin_b, pool_w_b, pool_scale3,
                                          layer=i, tt=min(TT_INPROJ, seq))
        attn, gates_b = _swa(q, k, v, attn_sinks, x, w_in_b, b_gate3, layer=i, tq=min(TQ_SWA, seq))
        bw, cw, are, aim = _ssm_weights(ssm_a_re[i], ssm_a_im[i], ssm_log_dt[i], ssm_b_re[i], ssm_b_im[i],
                                        ssm_c_re[i], ssm_c_im[i], bsz)
        hs = _ssm(u, bw, cw, are, aim, ssm_d3, wglu_b, layer=i, tc=TC_SSM)
        x2 = _merge(x.reshape(n, d), attn.reshape(n, -1), hs.reshape(n, -1), pm.reshape(n, -1),
                    gates_a.reshape(n, -1), gates_b.reshape(n, -1), merge_stacks, layer=i, tm=TM_MERGE)

        kv = _matmul(mem2, wkv_b, layer=i, tm=512, out_dtype=BF16).reshape(bsz, nmem, 2 * d)
        x3 = _xattn(x2.reshape(bsz, seq, d), kv, xattn_stacks, layer=i, tm=min(TM_XATTN, seq)).reshape(n, d)

        j = i // 2
        if i % 2 == 0:
            x4, expert_gu_b, expert_down_b = _ffn(x3, ffn_gu_b, ffn_down_b, ln3_g3, ln3_b3, moe_w_gu, moe_w_down,
                                                  layer=i, ffn_layer=j, tm=TM_FFN, nchunk=FFN_CHUNKS)
        else:
            x4 = _moe(x3, router_t, router_b, expert_gu_b, expert_down_b, ln3_g3, ln3_b3, layer=i, moe_layer=j)
        x = x4.reshape(bsz, seq, d)
    return x
```

```python
import functools

import jax
import jax.numpy as jnp
from jax import lax
from jax.experimental import pallas as pl
from jax.experimental.pallas import tpu as pltpu

D_MODEL = 1024
DEPTH = 4
N_Q_HEADS = 8
N_KV_HEADS = 2
HEAD_DIM = 64
WINDOW = 128
ROT_DIM = HEAD_DIM // 4
ROPE_THETA = 500000.0
SSM_WIDTH = D_MODEL // 2
SSM_GROUP = 16
SSM_GROUPS = SSM_WIDTH // SSM_GROUP
SSM_STATE = 64
SSM_LANES = SSM_GROUPS * SSM_STATE
POOL_WINDOWS = (2, 4, 8, 16)
POOL_WIDTH = D_MODEL // 2
POOL_GROUP = POOL_WIDTH // len(POOL_WINDOWS)
POOL_HALO = 16
X_HEADS = 4
X_HEAD_DIM = D_MODEL // X_HEADS
D_FF = 2816
N_EXPERTS = 8
D_FF_EXPERT = 3584
Q_WIDTH = N_Q_HEADS * HEAD_DIM
KV_WIDTH = N_KV_HEADS * HEAD_DIM
QKV_WIDTH = Q_WIDTH + 2 * KV_WIDTH
BRANCH_IN_WIDTH = QKV_WIDTH + SSM_WIDTH + POOL_WIDTH
N_BRANCHES = 3
GATES_IN_INPROJ = 2
DN_ALPHA = (2.0 * DEPTH) ** 0.25
LN_EPS = 1e-5
NEG_INF = -1e30

TT_INPROJ = 1024
TQ_SWA = 1024
TC_SSM = 128
TM_MERGE = 1024
TM_XATTN = 1024
TM_FFN = 512
FFN_CHUNKS = 2

LANE = 128
MXU_TILE = 256
VMEM_LIMIT = 56 * 1024 * 1024

BF16 = jnp.bfloat16
F32 = jnp.float32


def _dot(a, b):
    return jnp.dot(a, b, preferred_element_type=F32)


def _dot_nt(a, b):
    return lax.dot_general(a, b, (((1,), (1,)), ((), ())), preferred_element_type=F32)


def _layer_norm(y, g, b):
    mu = jnp.mean(y, axis=-1, keepdims=True)
    d = y - mu
    var = jnp.mean(d * d, axis=-1, keepdims=True)
    return d * lax.rsqrt(var + LN_EPS) * g + b


def _branch_gates(xb, w_ref, bg_ref, gate_ref, first, count):
    for i in range(count):
        cols = slice((first + i) * D_MODEL, (first + i + 1) * D_MODEL)
        wcols = slice(BRANCH_IN_WIDTH + cols.start, BRANCH_IN_WIDTH + cols.stop)
        gate_ref[:, i * D_MODEL:(i + 1) * D_MODEL] = (
            jax.nn.sigmoid(_dot(xb, w_ref[:, wcols]) + bg_ref[:, cols]).astype(BF16))


def _const_spec(shape):
    nd = len(shape)
    return pl.BlockSpec(shape, lambda *_: (0,) * nd, pipeline_mode=pl.Buffered(1))


def _layer_spec(stack, layer, width=None, col=0):
    rest = stack.shape[1:]
    block = (None,) + rest[:-1] + (rest[-1] if width is None else width,)
    index = (layer,) + (0,) * (len(rest) - 1) + (col,)
    return pl.BlockSpec(block, lambda *_: index, pipeline_mode=pl.Buffered(1))


def _params(sem, limit=VMEM_LIMIT):
    return pltpu.CompilerParams(dimension_semantics=sem, vmem_limit_bytes=limit)


def _rotary(t, cos, sin_a, sin_b):
    return t * cos + pltpu.roll(t, LANE - ROT_DIM // 2, 1) * sin_a + pltpu.roll(t, ROT_DIM // 2, 1) * sin_b


def _inproj_kernel(x_ref, w_ref, bg_ref, cos_ref, sa_ref, sb_ref, pw_ref, ps_ref,
                   q_ref, k_ref, v_ref, u_ref, pm_ref, gate_ref, hist_ref, *, tt):
    ti = pl.program_id(1)
    xb = x_ref[...].astype(BF16)
    cos, sa, sb = cos_ref[...], sa_ref[...], sb_ref[...]

    qkv = _dot(xb, w_ref[:, 0:QKV_WIDTH])
    for c in range(Q_WIDTH // LANE):
        t = qkv[:, c * LANE:(c + 1) * LANE] * (HEAD_DIM ** -0.5)
        q_ref[:, c * LANE:(c + 1) * LANE] = _rotary(t, cos, sa, sb).astype(BF16)
    k_ref[...] = _rotary(qkv[:, Q_WIDTH:Q_WIDTH + KV_WIDTH], cos, sa, sb).astype(BF16)
    v_ref[...] = qkv[:, Q_WIDTH + KV_WIDTH:].astype(BF16)

    u_ref[...] = _dot(xb, w_ref[:, QKV_WIDTH:QKV_WIDTH + SSM_WIDTH])

    @pl.when(ti == 0)
    def _():
        hist_ref[0:POOL_HALO, :] = jnp.zeros((POOL_HALO, POOL_WIDTH), F32)

    up = _dot(xb, w_ref[:, QKV_WIDTH + SSM_WIDTH:BRANCH_IN_WIDTH])
    hist_ref[POOL_HALO:POOL_HALO + tt, :] = up
    pos1 = (ti * tt + 1 + lax.broadcasted_iota(jnp.int32, (tt, 1), 0)).astype(F32)
    for gi, w in enumerate(POOL_WINDOWS):
        lanes = slice(gi * POOL_GROUP, (gi + 1) * POOL_GROUP)
        acc = hist_ref[:, lanes]
        k = 1
        while k < w:
            acc = acc + pltpu.roll(acc, k, 0)
            k *= 2
        pooled = acc[POOL_HALO:, :] / jnp.minimum(pos1, float(w)) - up[:, lanes]
        mixed = _dot(pooled.astype(BF16), pw_ref[gi]) * ps_ref[:, lanes]
        pm_ref[:, lanes] = mixed.astype(BF16)
    hist_ref[0:POOL_HALO, :] = hist_ref[tt:tt + POOL_HALO, :]

    _branch_gates(xb, w_ref, bg_ref, gate_ref, 0, GATES_IN_INPROJ)


def _inproj(x, w_in, b_gate, cos, sa, sb, pool_w, pool_scale, *, layer, tt):
    bsz, seq, d = x.shape
    row = lambda w: pl.BlockSpec((None, tt, w), lambda b, t: (b, t, 0))
    tab = pl.BlockSpec((tt, LANE), lambda b, t: (t, 0))
    return pl.pallas_call(
        functools.partial(_inproj_kernel, tt=tt),
        grid=(bsz, seq // tt),
        in_specs=[row(d), _layer_spec(w_in, layer), _layer_spec(b_gate, layer), tab, tab, tab,
                  _layer_spec(pool_w, layer), _layer_spec(pool_scale, layer)],
        out_specs=[row(Q_WIDTH), row(KV_WIDTH), row(KV_WIDTH), row(SSM_WIDTH), row(POOL_WIDTH),
                   row(GATES_IN_INPROJ * D_MODEL)],
        out_shape=[jax.ShapeDtypeStruct((bsz, seq, Q_WIDTH), BF16),
                   jax.ShapeDtypeStruct((bsz, seq, KV_WIDTH), BF16),
                   jax.ShapeDtypeStruct((bsz, seq, KV_WIDTH), BF16),
                   jax.ShapeDtypeStruct((bsz, seq, SSM_WIDTH), F32),
                   jax.ShapeDtypeStruct((bsz, seq, POOL_WIDTH), BF16),
                   jax.ShapeDtypeStruct((bsz, seq, GATES_IN_INPROJ * D_MODEL), BF16)],
        scratch_shapes=[pltpu.VMEM((tt + POOL_HALO, POOL_WIDTH), F32)],
        compiler_params=_params(("arbitrary", "arbitrary")),
        name="inproj",
    )(x, w_in, b_gate, cos, sa, sb, pool_w, pool_scale)


def _half_lanes(a, b, lo):
    return jnp.where(lo, a, b)


def _swa_kernel(sink_ref, q_ref, k_ref, kp_ref, v_ref, vp_ref, x_ref, w_ref, bg_ref, o_ref, gate_ref, *, nsub):
    _branch_gates(x_ref[...].astype(BF16), w_ref, bg_ref, gate_ref, GATES_IN_INPROJ, N_BRANCHES - GATES_IN_INPROJ)

    ti = pl.program_id(1)
    blk = WINDOW
    lo = lax.broadcasted_iota(jnp.int32, (2 * blk, LANE), 1) < HEAD_DIM
    tri = (lax.broadcasted_iota(jnp.int32, (2 * blk, blk), 1)
           <= lax.broadcasted_iota(jnp.int32, (2 * blk, blk), 0) % blk)
    top = lax.broadcasted_iota(jnp.int32, (2 * blk, 1), 0) < blk
    zero = jnp.zeros((2 * blk, LANE), F32)
    ones_sel = ((lax.broadcasted_iota(jnp.int32, (4 * blk, LANE), 0) < 2 * blk)
                == (lax.broadcasted_iota(jnp.int32, (4 * blk, LANE), 1) < HEAD_DIM)).astype(BF16)

    for j in range(nsub):
        if j == 0:
            kc = jnp.concatenate([kp_ref[...], k_ref[0:blk, :]], axis=0)
            vc = jnp.concatenate([vp_ref[...], v_ref[0:blk, :]], axis=0)
        else:
            kc = k_ref[(j - 1) * blk:(j + 1) * blk, :]
            vc = v_ref[(j - 1) * blk:(j + 1) * blk, :]
        kc = kc.astype(F32)
        vc = vc.astype(F32)
        kr = pltpu.roll(kc, HEAD_DIM, 1)
        vr = pltpu.roll(vc, HEAD_DIM, 1)
        kmat = [[_half_lanes(kc, zero, lo), _half_lanes(zero, kr, lo)],
                [_half_lanes(kr, zero, lo), _half_lanes(zero, kc, lo)]]
        vmat = [[_half_lanes(vc, zero, lo), _half_lanes(zero, vr, lo)],
                [_half_lanes(vr, zero, lo), _half_lanes(zero, vc, lo)]]
        rows = pl.ds(j * blk, blk)
        for kh in range(N_KV_HEADS):
            base = kh * 2 * LANE
            qq = jnp.concatenate([q_ref[rows, base:base + LANE], q_ref[rows, base + LANE:base + 2 * LANE]], axis=0)
            probs, sink_term = [], []
            for par in range(2):
                s = _dot_nt(qq, kmat[kh][par].astype(BF16))
                s_prev = s[:, :blk] if j > 0 else jnp.where(ti > 0, s[:, :blk], NEG_INF)
                s = jnp.where(tri, s[:, blk:], s_prev)
                h_top = kh * 4 + par
                sink = jnp.where(top, sink_ref[h_top], sink_ref[h_top + 2])
                m = jnp.maximum(jnp.max(s, axis=-1, keepdims=True), sink)
                p = jnp.exp(s - m)
                sink_term.append(jnp.exp(sink - m))
                probs.append(jnp.concatenate([jnp.where(tri, 0.0, p), jnp.where(tri, p, 0.0)], axis=1).astype(BF16))
            pcat = jnp.concatenate(probs, axis=1)
            vcat = jnp.concatenate([vmat[kh][0], vmat[kh][1]], axis=0).astype(BF16)
            ov = _dot(pcat, jnp.concatenate([vcat, ones_sel], axis=1))
            den = ov[:, LANE:] + jnp.where(lo, sink_term[0], sink_term[1])
            o = ov[:, :LANE] / den
            o_ref[rows, base:base + LANE] = o[0:blk].astype(BF16)
            o_ref[rows, base + LANE:base + 2 * LANE] = o[blk:2 * blk].astype(BF16)


def _swa(q, k, v, sinks, x, w_in, b_gate, *, layer, tq):
    bsz, seq, _ = q.shape
    nsub = tq // WINDOW
    cur = lambda w: pl.BlockSpec((None, tq, w), lambda b, t: (b, t, 0))
    prev = pl.BlockSpec((None, WINDOW, KV_WIDTH), lambda b, t: (b, jnp.maximum(t * nsub - 1, 0), 0))
    return pl.pallas_call(
        functools.partial(_swa_kernel, nsub=nsub),
        grid=(bsz, seq // tq),
        in_specs=[pl.BlockSpec(memory_space=pltpu.SMEM), cur(Q_WIDTH), cur(KV_WIDTH), prev, cur(KV_WIDTH), prev,
                  cur(D_MODEL), _layer_spec(w_in, layer), _layer_spec(b_gate, layer)],
        out_specs=[cur(Q_WIDTH), cur((N_BRANCHES - GATES_IN_INPROJ) * D_MODEL)],
        out_shape=[jax.ShapeDtypeStruct((bsz, seq, Q_WIDTH), BF16),
                   jax.ShapeDtypeStruct((bsz, seq, (N_BRANCHES - GATES_IN_INPROJ) * D_MODEL), BF16)],
        compiler_params=_params(("parallel", "parallel")),
        name="swa",
    )(sinks[layer], q, k, k, v, v, x, w_in, b_gate)


SSM_CHUNKS = 4
SSM_CHUNK_LANES = SSM_LANES // SSM_CHUNKS
SSM_CHUNK_IN = SSM_WIDTH // SSM_CHUNKS
SCAN_LANES = 512


def _ssm_kernel(u_ref, bw_ref, cw_ref, are_ref, aim_ref, d_ref, wglu_ref, o_ref, xs_ref, st_ref, h_ref,
                *, tc, bsz):
    @pl.when(pl.program_id(0) == 0)
    def _():
        h_ref[...] = jnp.zeros(h_ref.shape, F32)

    for b in range(bsz):
        for c in range(SSM_CHUNKS):
            xs_ref[c, pl.ds(b, tc, stride=bsz), :] = u_ref[b, :, c * SSM_CHUNK_IN:(c + 1) * SSM_CHUNK_IN]

    for part in range(2):
        for c in range(SSM_CHUNKS):
            col = part * SSM_LANES + c * SSM_CHUNK_LANES
            st_ref[:, col:col + SSM_CHUNK_LANES] = _dot(xs_ref[c].astype(BF16), bw_ref[part * SSM_CHUNKS + c])

    for jb in range(SSM_LANES // SCAN_LANES):
        re = pl.ds(jb * SCAN_LANES, SCAN_LANES)
        im = pl.ds(SSM_LANES + jb * SCAN_LANES, SCAN_LANES)
        ar = are_ref[:, re]
        ai = aim_ref[:, re]

        hr, hi = h_ref[:, re], h_ref[:, im]
        for t in range(tc):
            r = pl.ds(t * bsz, bsz)
            hr, hi = ar * hr - ai * hi + st_ref[r, re], ar * hi + ai * hr + st_ref[r, im]
            st_ref[r, re] = hr
            st_ref[r, im] = hi
        h_ref[:, re] = hr
        h_ref[:, im] = hi

    ys = []
    for c in range(SSM_CHUNKS):
        sre = st_ref[:, c * SSM_CHUNK_LANES:(c + 1) * SSM_CHUNK_LANES].astype(BF16)
        sim = st_ref[:, SSM_LANES + c * SSM_CHUNK_LANES:SSM_LANES + (c + 1) * SSM_CHUNK_LANES].astype(BF16)
        cols = slice(c * SSM_CHUNK_IN, (c + 1) * SSM_CHUNK_IN)
        y = _dot(sre, cw_ref[c]) + _dot(sim, cw_ref[SSM_CHUNKS + c])
        ys.append(y + d_ref[:, cols] * xs_ref[c])

    hh = jax.nn.gelu(jnp.concatenate(ys, axis=1)).astype(BF16)
    z = _dot(hh, wglu_ref[...])
    res = z[:, :SSM_WIDTH] * jax.nn.sigmoid(z[:, SSM_WIDTH:])
    for c in range(SSM_CHUNKS):
        xs_ref[c] = res[:, c * SSM_CHUNK_IN:(c + 1) * SSM_CHUNK_IN]
    for b in range(bsz):
        for c in range(SSM_CHUNKS):
            o_ref[b, :, c * SSM_CHUNK_IN:(c + 1) * SSM_CHUNK_IN] = (
                xs_ref[c, pl.ds(b, tc, stride=bsz), :].astype(BF16))


def _ssm(u, bw, cw, are, aim, d_skip, wglu, *, layer, tc):
    bsz, seq, _ = u.shape
    blk = pl.BlockSpec((bsz, tc, SSM_WIDTH), lambda t: (0, t, 0))
    return pl.pallas_call(
        functools.partial(_ssm_kernel, tc=tc, bsz=bsz),
        grid=(seq // tc,),
        in_specs=[blk, _const_spec(bw.shape), _const_spec(cw.shape), _const_spec(are.shape),
                  _const_spec(aim.shape), _layer_spec(d_skip, layer), _layer_spec(wglu, layer)],
        out_specs=blk,
        out_shape=jax.ShapeDtypeStruct((bsz, seq, SSM_WIDTH), BF16),
        scratch_shapes=[pltpu.VMEM((SSM_CHUNKS, tc * bsz, SSM_CHUNK_IN), F32),
                        pltpu.VMEM((tc * bsz, 2 * SSM_LANES), F32),
                        pltpu.VMEM((bsz, 2 * SSM_LANES), F32)],
        compiler_params=_params(("arbitrary",)),
        name="ssm",
    )(u, bw, cw, are, aim, d_skip, wglu)


def _ssm_weights(a_re, a_im, log_dt, b_re, b_im, c_re, c_im, bsz):
    dt = jnp.exp(log_dt)[:, None]
    decay = jnp.exp(dt * a_re)
    abar_re, abar_im = decay * jnp.cos(dt * a_im), decay * jnp.sin(dt * a_im)
    inv_abs2 = 1.0 / (a_re * a_re + a_im * a_im)
    num_re, num_im = abar_re - 1.0, abar_im
    f_re = (num_re * a_re + num_im * a_im) * inv_abs2
    f_im = (num_im * a_re - num_re * a_im) * inv_abs2
    bbar_re = f_re[..., None] * b_re - f_im[..., None] * b_im
    bbar_im = f_re[..., None] * b_im + f_im[..., None] * b_re
    gpc = SSM_GROUPS // SSM_CHUNKS
    eye = jnp.eye(gpc, dtype=F32)

    def in_chunks(bb):
        t = bb.reshape(SSM_CHUNKS, gpc, SSM_STATE, SSM_GROUP)
        return jnp.einsum('cgnp,gh->cgphn', t, eye).reshape(SSM_CHUNKS, gpc * SSM_GROUP, gpc * SSM_STATE)

    def out_chunks(cc):
        t = cc.reshape(SSM_CHUNKS, gpc, SSM_GROUP, SSM_STATE)
        return jnp.einsum('cgpn,gh->cgnhp', t, eye).reshape(SSM_CHUNKS, gpc * SSM_STATE, gpc * SSM_GROUP)

    bw = jnp.concatenate([in_chunks(bbar_re), in_chunks(bbar_im)], axis=0).astype(BF16)
    cw = jnp.concatenate([out_chunks(c_re), out_chunks(-c_im)], axis=0).astype(BF16)
    are = jnp.broadcast_to(abar_re.reshape(1, SSM_LANES), (bsz, SSM_LANES))
    aim = jnp.broadcast_to(abar_im.reshape(1, SSM_LANES), (bsz, SSM_LANES))
    return bw, cw, are, aim


def _merge_kernel(x_ref, a_ref, s_ref, p_ref, ga_ref, gb_ref, wa_ref, ws_ref, wp_ref, wo_ref, g_ref, b_ref, o_ref):
    merged = None
    for i, (br_ref, w_ref) in enumerate(((a_ref, wa_ref), (s_ref, ws_ref), (p_ref, wp_ref))):
        gate_ref, gi = (ga_ref, i) if i < GATES_IN_INPROJ else (gb_ref, i - GATES_IN_INPROJ)
        gate = gate_ref[:, gi * D_MODEL:(gi + 1) * D_MODEL].astype(F32)
        term = gate * _dot(br_ref[...], w_ref[...])
        merged = term if merged is None else merged + term
    h = _dot(merged.astype(BF16), wo_ref[...])
    o_ref[...] = _layer_norm(DN_ALPHA * x_ref[...] + h, g_ref[...], b_ref[...])


def _merge(x, a, s, p, gates_a, gates_b, stacks, *, layer, tm):
    n, d = x.shape
    row = lambda w: pl.BlockSpec((tm, w), lambda i: (i, 0))
    return pl.pallas_call(
        _merge_kernel,
        grid=(n // tm,),
        in_specs=([row(d), row(Q_WIDTH), row(SSM_WIDTH), row(POOL_WIDTH), row(gates_a.shape[1]), row(gates_b.shape[1])]
                  + [_layer_spec(c, layer) for c in stacks]),
        out_specs=row(d),
        out_shape=jax.ShapeDtypeStruct((n, d), F32),
        compiler_params=_params(("parallel",)),
        name="merge",
    )(x, a, s, p, gates_a, gates_b, *stacks)


def _matmul_kernel(a_ref, w_ref, o_ref):
    o_ref[...] = _dot(a_ref[...].astype(BF16), w_ref[...]).astype(o_ref.dtype)


def _kv_projections(mem, wkv, *, tm):
    m, k = mem.shape
    layers, _, n = wkv.shape
    return pl.pallas_call(
        _matmul_kernel,
        grid=(layers, m // tm),
        in_specs=[pl.BlockSpec((tm, k), lambda l, i: (i, 0)), pl.BlockSpec((None, k, n), lambda l, i: (l, 0, 0))],
        out_specs=pl.BlockSpec((None, tm, n), lambda l, i: (l, i, 0)),
        out_shape=jax.ShapeDtypeStruct((layers, m, n), BF16),
        compiler_params=_params(("parallel", "parallel")),
        name="kvproj",
    )(mem, wkv)


def _xattn_kernel(x_ref, k_ref, v_ref, wq_ref, wo_ref, g_ref, b_ref, o_ref):
    x = x_ref[...]
    q = (_dot(x.astype(BF16), wq_ref[...]) * (X_HEAD_DIM ** -0.5)).astype(BF16)
    outs = []
    for h in range(X_HEADS):
        cols = slice(h * X_HEAD_DIM, (h + 1) * X_HEAD_DIM)
        s = _dot_nt(q[:, cols], k_ref[:, cols])
        p = jnp.exp(s - jnp.max(s, axis=-1, keepdims=True))
        inv = 1.0 / jnp.sum(p, axis=-1, keepdims=True)
        outs.append((_dot(p.astype(BF16), v_ref[:, cols]) * inv).astype(BF16))
    c = _dot(jnp.concatenate(outs, axis=1), wo_ref[...])
    o_ref[...] = _layer_norm(DN_ALPHA * x + c, g_ref[...], b_ref[...])


def _xattn(x, kv, stacks, *, layer, tm):
    bsz, seq, d = x.shape
    nmem = kv.shape[2]
    row = pl.BlockSpec((None, tm, d), lambda bi, t: (bi, t, 0))
    kspec = pl.BlockSpec((None, None, nmem, d), lambda bi, t: (layer, bi, 0, 0))
    vspec = pl.BlockSpec((None, None, nmem, d), lambda bi, t: (layer, bi, 0, 1))
    return pl.pallas_call(
        _xattn_kernel,
        grid=(bsz, seq // tm),
        in_specs=[row, kspec, vspec] + [_layer_spec(c, layer) for c in stacks],
        out_specs=row,
        out_shape=jax.ShapeDtypeStruct((bsz, seq, d), F32),
        compiler_params=_params(("parallel", "parallel")),
        name="xattn",
    )(x, kv, kv, *stacks)


def _ff_chunks(width, nchunk):
    tiles = width // MXU_TILE
    assert tiles * MXU_TILE == width
    bounds = [MXU_TILE * ((tiles * c) // nchunk) for c in range(nchunk + 1)]
    return [slice(lo, hi) for lo, hi in zip(bounds[:-1], bounds[1:])]


def _swiglu(xb, wg_ref, wu_ref, wd_ref, nchunk):
    acc = None
    for cols in _ff_chunks(wg_ref.shape[1], nchunk):
        hid = jax.nn.silu(_dot(xb, wg_ref[:, cols])) * _dot(xb, wu_ref[:, cols])
        part = _dot(hid.astype(BF16), wd_ref[cols, :])
        acc = part if acc is None else acc + part
    return acc


def _ffn_kernel(x_ref, wg_ref, wu_ref, wd_ref, g_ref, b_ref, egu_ref, edn_ref, o_ref, egu_out_ref, edn_out_ref,
                *, nchunk):
    x = x_ref[...]
    acc = _swiglu(x.astype(BF16), wg_ref, wu_ref, wd_ref, nchunk)
    o_ref[...] = _layer_norm(DN_ALPHA * x + acc, g_ref[...], b_ref[...])
    egu_out_ref[...] = egu_ref[...].astype(BF16)
    edn_out_ref[...] = edn_ref[...].astype(BF16)


def _ffn(x, wgu, wd, g, b, expert_gu, expert_down, *, layer, ffn_layer, tm, nchunk):
    n, d = x.shape
    ff = wd.shape[1]
    steps = n // tm
    _, ne, gu_rows, gu_cols = expert_gu.shape
    _, _, dn_rows, dn_cols = expert_down.shape
    per_expert = steps // ne
    assert per_expert * ne == steps

    def slices(rows, cols):
        blk = (None, None, rows // per_expert, cols)
        return (pl.BlockSpec(blk, lambda i: (ffn_layer, i // per_expert, i % per_expert, 0)),
                pl.BlockSpec(blk, lambda i: (0, i // per_expert, i % per_expert, 0)))

    gu_in, gu_out = slices(gu_rows, gu_cols)
    dn_in, dn_out = slices(dn_rows, dn_cols)
    row = pl.BlockSpec((tm, d), lambda i: (i, 0))
    return pl.pallas_call(
        functools.partial(_ffn_kernel, nchunk=nchunk),
        grid=(steps,),
        in_specs=[row, _layer_spec(wgu, ffn_layer, width=ff, col=0), _layer_spec(wgu, ffn_layer, width=ff, col=1),
                  _layer_spec(wd, ffn_layer), _layer_spec(g, layer), _layer_spec(b, layer), gu_in, dn_in],
        out_specs=[row, gu_out, dn_out],
        out_shape=[jax.ShapeDtypeStruct((n, d), F32),
                   jax.ShapeDtypeStruct((1,) + expert_gu.shape[1:], BF16),
                   jax.ShapeDtypeStruct((1,) + expert_down.shape[1:], BF16)],
        compiler_params=_params(("parallel",)),
        name="ffn",
    )(x, wgu, wgu, wd, g, b, expert_gu, expert_down)


MOE_TT = 512
SEG_ALIGN = 16
SEG_BIG = 128
MOE_RB = 512
MOE_RTILE = 2 * MOE_TT + N_EXPERTS * SEG_ALIGN
MOE_FF_CHUNKS = 2
R_D1, R_D2, R_W1, R_W2 = range(4)


def _router_kernel(x_ref, wr_ref, br_ref, before_ref, route_ref, routet_ref, cnt_ref, *, tm):
    x, wr = x_ref[...], wr_ref[...]
    x_hi, wr_hi = x.astype(BF16), wr.astype(BF16)
    x_lo, wr_lo = (x - x_hi.astype(F32)).astype(BF16), (wr - wr_hi.astype(F32)).astype(BF16)
    logits = _dot_nt(wr_hi, x_hi) + _dot_nt(wr_hi, x_lo) + _dot_nt(wr_lo, x_hi) + br_ref[...]
    eid = lax.broadcasted_iota(jnp.int32, logits.shape, 0)
    v1 = jnp.max(logits, axis=0, keepdims=True)
    e1 = jnp.min(jnp.where(logits == v1, eid, N_EXPERTS), axis=0, keepdims=True)
    rest = jnp.where(eid == e1, -jnp.inf, logits)
    v2 = jnp.max(rest, axis=0, keepdims=True)
    e2 = jnp.min(jnp.where(rest == v2, eid, N_EXPERTS), axis=0, keepdims=True)
    t = jnp.exp(v2 - v1)
    w1 = 1.0 / (1.0 + t)
    w2 = t / (1.0 + t)
    sel1, sel2 = eid == e1, eid == e2
    member = (sel1 | sel2).astype(BF16)
    rank = _dot(member, before_ref[...])
    cnt = jnp.sum(member.astype(F32), axis=1, keepdims=True)
    cnt_ref[...] = jnp.broadcast_to(cnt, (N_EXPERTS, LANE))
    padded_cnt = jnp.floor((cnt + (SEG_ALIGN - 1)) * (1.0 / SEG_ALIGN)) * SEG_ALIGN
    seg = jnp.broadcast_to(padded_cnt, logits.shape)
    d1 = jnp.sum(jnp.where(eid < e1, seg, 0.0) + jnp.where(sel1, rank, 0.0), axis=0, keepdims=True)
    d2 = jnp.sum(jnp.where(eid < e2, seg, 0.0) + jnp.where(sel2, rank, 0.0), axis=0, keepdims=True)
    route = jnp.zeros(logits.shape, F32)
    for r, val in ((R_D1, d1), (R_D2, d2), (R_W1, w1), (R_W2, w2)):
        route = jnp.where(eid == r, val, route)
    route_ref[...] = route
    padded = jnp.concatenate([route, jnp.zeros((LANE - N_EXPERTS, tm), F32)], axis=0)
    routet_ref[...] = padded.T


def _router(x, wr_t, br, *, layer, tm):
    n, d = x.shape
    nt = n // tm
    before = (jnp.arange(tm)[:, None] < jnp.arange(tm)[None, :]).astype(BF16)
    return pl.pallas_call(
        functools.partial(_router_kernel, tm=tm),
        grid=(nt,),
        in_specs=[pl.BlockSpec((tm, d), lambda i: (i, 0)), _layer_spec(wr_t, layer), _layer_spec(br, layer),
                  _const_spec(before.shape)],
        out_specs=[pl.BlockSpec((N_EXPERTS, tm), lambda i: (0, i)),
                   pl.BlockSpec((tm, LANE), lambda i: (i, 0)),
                   pl.BlockSpec((None, N_EXPERTS, LANE), lambda i: (i, 0, 0))],
        out_shape=[jax.ShapeDtypeStruct((N_EXPERTS, n), F32),
                   jax.ShapeDtypeStruct((n, LANE), F32),
                   jax.ShapeDtypeStruct((nt, N_EXPERTS, LANE), F32)],
        compiler_params=_params(("parallel",)),
        name="router",
    )(x, wr_t, br, before)


def _start_segments(pc_ref, off_ref, tile, make_copy):
    loc = jnp.int32(0)
    for e in range(N_EXPERTS):
        rows = pc_ref[tile * N_EXPERTS + e]
        glob = off_ref[tile * N_EXPERTS + e]
        nbig = rows // SEG_BIG
        for size, first, count in ((SEG_BIG, 0, nbig), (SEG_ALIGN, nbig * SEG_BIG, (rows % SEG_BIG) // SEG_ALIGN)):
            def start(p, carry, loc=loc + first, glob=glob + first, size=size):
                make_copy(pl.multiple_of(loc + p * size, SEG_ALIGN),
                          pl.multiple_of(glob + p * size, SEG_ALIGN), size).start()
                return carry

            lax.fori_loop(0, count, start, 0)
        loc = loc + rows


def _wait_segments(pc_ref, tile, make_copy):
    nbig = nsmall = jnp.int32(0)
    for e in range(N_EXPERTS):
        rows = pc_ref[tile * N_EXPERTS + e]
        nbig = nbig + rows // SEG_BIG
        nsmall = nsmall + (rows % SEG_BIG) // SEG_ALIGN
    for size, count in ((SEG_BIG, nbig), (SEG_ALIGN, nsmall)):
        lax.fori_loop(0, count, lambda p, carry, size=size: (make_copy(0, 0, size).wait(), carry)[1], 0)


def _dispatch_kernel(pc_ref, off_ref, tail_ref, nused_ref, x_ref, route_ref, buf_ref, stage_ref, zero_ref, sem_ref,
                     *, nblocks):
    i = pl.program_id(0)
    last = pl.num_programs(0) - 1
    slot = i % 2
    d1 = route_ref[R_D1:R_D1 + 1, :].astype(jnp.int32)
    d2 = route_ref[R_D2:R_D2 + 1, :].astype(jnp.int32)
    rows = lax.broadcasted_iota(jnp.int32, (MOE_RTILE, MOE_TT), 0)
    onehot = ((rows == d1) | (rows == d2)).astype(BF16)
    stage_ref[slot] = _dot(onehot, x_ref[...].astype(BF16)).astype(BF16)

    def seg_copy(s):
        return lambda loc, glob, size: pltpu.make_async_copy(
            stage_ref.at[s, pl.ds(loc, size), :], buf_ref.at[pl.ds(glob, size), :], sem_ref.at[s])

    _start_segments(pc_ref, off_ref, i, seg_copy(slot))

    @pl.when(i > 0)
    def _():
        _wait_segments(pc_ref, i - 1, seg_copy(1 - slot))

    @pl.when(i == last)
    def _():
        _wait_segments(pc_ref, i, seg_copy(slot))
        zero_ref[...] = jnp.zeros(zero_ref.shape, BF16)

        def piece(glob):
            return pltpu.make_async_copy(zero_ref.at[pl.ds(0, SEG_ALIGN), :],
                                         buf_ref.at[pl.ds(glob, SEG_ALIGN), :], sem_ref.at[2])

        def block(row):
            return pltpu.make_async_copy(zero_ref, buf_ref.at[pl.ds(row, MOE_RB), :], sem_ref.at[2])

        npieces = jnp.int32(0)
        for e in range(N_EXPERTS):
            start, cnt = tail_ref[e], tail_ref[N_EXPERTS + e]
            lax.fori_loop(0, cnt, lambda p, c, start=start: (
                piece(pl.multiple_of(start + p * SEG_ALIGN, SEG_ALIGN)).start(), c)[1], 0)
            npieces = npieces + cnt
        lax.fori_loop(0, npieces, lambda p, c: (piece(0).wait(), c)[1], 0)
        lax.fori_loop(nused_ref[0], nblocks,
                      lambda b, c: (block(pl.multiple_of(b * MOE_RB, MOE_RB)).start(), c)[1], 0)
        lax.fori_loop(nused_ref[0], nblocks, lambda b, c: (block(0).wait(), c)[1], 0)


def _dispatch(x, route, pc, off, tail, nused, rmax):
    n, d = x.shape
    return pl.pallas_call(
        functools.partial(_dispatch_kernel, nblocks=rmax // MOE_RB),
        grid_spec=pltpu.PrefetchScalarGridSpec(
            num_scalar_prefetch=4,
            grid=(n // MOE_TT,),
            in_specs=[pl.BlockSpec((MOE_TT, d), lambda i, *_: (i, 0)),
                      pl.BlockSpec((N_EXPERTS, MOE_TT), lambda i, *_: (0, i))],
            out_specs=pl.BlockSpec(memory_space=pl.ANY),
            scratch_shapes=[pltpu.VMEM((2, MOE_RTILE, d), BF16), pltpu.VMEM((MOE_RB, d), BF16),
                            pltpu.SemaphoreType.DMA((3,))]),
        out_shape=jax.ShapeDtypeStruct((rmax, d), BF16),
        compiler_params=_params(("arbitrary",)),
        name="dispatch",
    )(pc, off, tail, nused, x, route)


def _experts_kernel(be_ref, nused_ref, x_ref, wg_ref, wu_ref, wd_ref, o_ref):
    @pl.when(pl.program_id(0) < nused_ref[0])
    def _():
        o_ref[...] = _swiglu(x_ref[...], wg_ref, wu_ref, wd_ref, MOE_FF_CHUNKS).astype(BF16)

    @pl.when(pl.program_id(0) >= nused_ref[0])
    def _():
        o_ref[...] = jnp.zeros(o_ref.shape, BF16)


def _experts(buf, be, nused, wgu, wd, *, moe_layer):
    rmax, d = buf.shape
    ff = wd.shape[2]
    blk = pl.BlockSpec((MOE_RB, d), lambda i, be, nu: (jnp.maximum(jnp.minimum(i, nu[0] - 1), 0), 0))
    out_blk = pl.BlockSpec((MOE_RB, d), lambda i, be, nu: (i, 0))
    once = pl.Buffered(1)
    return pl.pallas_call(
        _experts_kernel,
        grid_spec=pltpu.PrefetchScalarGridSpec(
            num_scalar_prefetch=2,
            grid=(rmax // MOE_RB,),
            in_specs=[blk,
                      pl.BlockSpec((None, None, d, ff), lambda i, be, nu: (moe_layer, be[i], 0, 0), pipeline_mode=once),
                      pl.BlockSpec((None, None, d, ff), lambda i, be, nu: (moe_layer, be[i], 0, 1), pipeline_mode=once),
                      pl.BlockSpec((None, None, ff, d), lambda i, be, nu: (moe_layer, be[i], 0, 0), pipeline_mode=once)],
            out_specs=out_blk),
        out_shape=jax.ShapeDtypeStruct((rmax, d), BF16),
        compiler_params=_params(("arbitrary",)),
        name="experts",
    )(be, nused, buf, wgu, wgu, wd)


def _combine_kernel(pc_ref, off_ref, x_ref, routet_ref, g_ref, b_ref, buf_ref, o_ref, stage_ref, sem_ref):
    i = pl.program_id(0)
    last = pl.num_programs(0) - 1
    slot = i % 2

    def seg_copy(s):
        return lambda loc, glob, size: pltpu.make_async_copy(
            buf_ref.at[pl.ds(glob, size), :], stage_ref.at[s, pl.ds(loc, size), :], sem_ref.at[s])

    @pl.when(i == 0)
    def _():
        stage_ref[...] = jnp.zeros(stage_ref.shape, BF16)
        _start_segments(pc_ref, off_ref, i, seg_copy(slot))

    _wait_segments(pc_ref, i, seg_copy(slot))

    @pl.when(i < last)
    def _():
        _start_segments(pc_ref, off_ref, i + 1, seg_copy(1 - slot))

    rt = routet_ref[...]
    col = lambda r: rt[:, r:r + 1]
    d1 = col(R_D1).astype(jnp.int32)
    d2 = col(R_D2).astype(jnp.int32)
    lanes = lax.broadcasted_iota(jnp.int32, (MOE_TT, MOE_RTILE), 1)
    y = stage_ref[slot]
    select = jnp.where(lanes == d1, col(R_W1), jnp.where(lanes == d2, col(R_W2), 0.0))
    f = _dot(select.astype(BF16), y)
    o_ref[...] = _layer_norm(DN_ALPHA * x_ref[...] + f, g_ref[...], b_ref[...])


def _combine(x, routet, pc, off, buf, g, b, *, layer):
    n, d = x.shape
    row = pl.BlockSpec((MOE_TT, d), lambda i, *_: (i, 0))
    return pl.pallas_call(
        _combine_kernel,
        grid_spec=pltpu.PrefetchScalarGridSpec(
            num_scalar_prefetch=2,
            grid=(n // MOE_TT,),
            in_specs=[row, pl.BlockSpec((MOE_TT, LANE), lambda i, *_: (i, 0)),
                      _layer_spec(g, layer), _layer_spec(b, layer), pl.BlockSpec(memory_space=pl.ANY)],
            out_specs=row,
            scratch_shapes=[pltpu.VMEM((2, MOE_RTILE, d), BF16), pltpu.SemaphoreType.DMA((2,))]),
        out_shape=jax.ShapeDtypeStruct((n, d), F32),
        compiler_params=_params(("arbitrary",)),
        name="combine",
    )(pc, off, x, routet, g, b, buf)


def _moe(x, wr_t, b_router, wgu, wd, g, b, *, layer, moe_layer):
    n, d = x.shape
    nt = n // MOE_TT
    route, routet, cnt = _router(x, wr_t, b_router, layer=moe_layer, tm=MOE_TT)

    cnt = cnt[:, :, 0].astype(jnp.int32)
    pc = (cnt + SEG_ALIGN - 1) // SEG_ALIGN * SEG_ALIGN
    total = jnp.sum(pc, axis=0)
    region = (total + MOE_RB - 1) // MOE_RB * MOE_RB
    base = jnp.cumsum(region) - region
    off = base[None, :] + jnp.cumsum(pc, axis=0) - pc
    rmax = (2 * n + nt * N_EXPERTS * (SEG_ALIGN - 1) + N_EXPERTS * (MOE_RB - 1) + MOE_RB - 1) // MOE_RB * MOE_RB
    blk_end = jnp.cumsum(region // MOE_RB)
    nused = blk_end[-1:].astype(jnp.int32)
    blk = jnp.minimum(jnp.arange(rmax // MOE_RB, dtype=jnp.int32), nused - 1)
    be = jnp.sum(blk[:, None] >= blk_end[None, :], axis=1).astype(jnp.int32)
    tail = jnp.concatenate([base + total, (region - total) // SEG_ALIGN]).astype(jnp.int32)
    pc, off = pc.reshape(-1), off.reshape(-1).astype(jnp.int32)

    buf = _dispatch(x, route, pc, off, tail, nused, rmax)
    y = _experts(buf, be, nused, wgu, wd, moe_layer=0)
    return _combine(x, routet, pc, off, y, g, b, layer=layer)


def _rotary_tables(seq):
    half = ROT_DIM // 2
    inv_freq = jnp.power(jnp.float32(ROPE_THETA), -jnp.arange(half, dtype=F32) / half)
    ang = jnp.arange(seq, dtype=jnp.int32).astype(F32)[:, None] * inv_freq[None, :]
    cos, sin = jnp.cos(ang), jnp.sin(ang)
    d = jnp.arange(LANE) % HEAD_DIM
    first, second = d < half, (d >= half) & (d < ROT_DIM)
    idx = d % half
    cos_t = jnp.where((first | second)[None, :], cos[:, idx], 1.0)
    sin_a = jnp.where(first[None, :], -sin[:, idx], 0.0)
    sin_b = jnp.where(second[None, :], sin[:, idx], 0.0)
    return cos_t, sin_a, sin_b


def kernel(x, mem, w_in, b_gate, attn_sinks, ssm_a_re, ssm_a_im, ssm_log_dt, ssm_b_re, ssm_b_im, ssm_c_re, ssm_c_im, ssm_d, ssm_w_glu, pool_w, pool_scale, w_br_attn, w_br_ssm, w_br_pool, w_o, ln1_g, ln1_b, xa_wq, xa_wkv, xa_wo, ln2_g, ln2_b, ffn_w_gu, ffn_w_down, moe_w_router, moe_b_router, moe_w_gu, moe_w_down, ln3_g, ln3_b):
    bsz, seq, d = x.shape
    n = bsz * seq
    nmem = mem.shape[1]
    cos_t, sin_a, sin_b = _rotary_tables(seq)
    mem2 = mem.reshape(bsz * nmem, d)
    bf = lambda w: w.astype(BF16)
    vec = lambda v: v.reshape(v.shape[0], 1, v.shape[1])

    w_in_b, pool_w_b, wglu_b = bf(w_in), bf(pool_w), bf(ssm_w_glu)
    b_gate3 = vec(b_gate)
    merge_stacks = (bf(w_br_attn), bf(w_br_ssm), bf(w_br_pool), bf(w_o), vec(ln1_g), vec(ln1_b))
    xattn_stacks = (bf(xa_wq), bf(xa_wo), vec(ln2_g), vec(ln2_b))
    kv_all = _kv_projections(mem2, bf(xa_wkv), tm=512).reshape(-1, bsz, nmem, 2 * d)
    ffn_gu_b, ffn_down_b = bf(ffn_w_gu), bf(ffn_w_down)
    router_t = jnp.swapaxes(moe_w_router, 1, 2)
    router_b = moe_b_router.reshape(moe_b_router.shape[0], N_EXPERTS, 1)
    pool_scale3, ssm_d3, ln3_g3, ln3_b3 = vec(pool_scale), vec(ssm_d), vec(ln3_g), vec(ln3_b)

    for i in range(DEPTH):
        q, k, v, u, pm, gates_a = _inproj(x, w_in_b, b_gate3, cos_t, sin_a, sin_b, pool_w_b, pool_scale3,
                                          layer=i, tt=min(TT_INPROJ, seq))
        attn, gates_b = _swa(q, k, v, attn_sinks, x, w_in_b, b_gate3, layer=i, tq=min(TQ_SWA, seq))
        bw, cw, are, aim = _ssm_weights(ssm_a_re[i], ssm_a_im[i], ssm_log_dt[i], ssm_b_re[i], ssm_b_im[i],
                                        ssm_c_re[i], ssm_c_im[i], bsz)
        hs = _ssm(u, bw, cw, are, aim, ssm_d3, wglu_b, layer=i, tc=TC_SSM)
        x2 = _merge(x.reshape(n, d), attn.reshape(n, -1), hs.reshape(n, -1), pm.reshape(n, -1),
                    gates_a.reshape(n, -1), gates_b.reshape(n, -1), merge_stacks, layer=i, tm=TM_MERGE)

        x3 = _xattn(x2.reshape(bsz, seq, d), kv_all, xattn_stacks, layer=i, tm=min(TM_XATTN, seq)).reshape(n, d)

        j = i // 2
        if i % 2 == 0:
            x4, expert_gu_b, expert_down_b = _ffn(x3, ffn_gu_b, ffn_down_b, ln3_g3, ln3_b3, moe_w_gu, moe_w_down,
                                                  layer=i, ffn_layer=j, tm=TM_FFN, nchunk=FFN_CHUNKS)
        else:
            x4 = _moe(x3, router_t, router_b, expert_gu_b, expert_down_b, ln3_g3, ln3_b3, layer=i, moe_layer=j)
        x = x4.reshape(bsz, seq, d)
    return x
```
